```python
import math
import jax
import jax.numpy as jnp
from jax import lax
import numpy as np

D_MODEL = 1024
BATCH = 32
SEQ = 2048
DEPTH = 1
DEC_BATCH = 8
DEC_SEQ = 16
PAST_LEN = 2048

CHUNK = 64
N_HEADS_A = 8
DK_A = 128
DV_A = 128
CONV_W = 4
N_HEADS_B = 8
DK_B = 128
DV_B = 128
HGRN_BLOCK = 16
D_FF = -(-8 * D_MODEL // (3 * 256)) * 256
WA_QK = N_HEADS_A * DK_A
WA_V = N_HEADS_A * DV_A
WB_K = N_HEADS_B * DK_B
WB_V = N_HEADS_B * DV_B
CONV_CH = 2 * WA_QK + WA_V
IN_SIZES = (WA_QK, WA_QK, WA_V, WA_V, N_HEADS_A, N_HEADS_A, WB_K, WB_K, WB_V, WB_V, D_MODEL, D_MODEL)
IN_TOTAL = sum(IN_SIZES)
ALPHA = (2.0 * DEPTH) ** 0.25
BETA_INIT = (8.0 * DEPTH) ** -0.25
LN_EPS = 1e-5
RMS_EPS = 1e-6
L2_EPS = 1e-6

kernel_name = 'hybrid_gdn_hgrn2_streaming_step'


def _split_points():
    return [int(s) for s in np.cumsum(IN_SIZES)[:-1]]


def _layer_norm(x, g, b):
    xf = x.astype(jnp.float32)
    mu = jnp.mean(xf, -1, keepdims=True)
    var = jnp.mean(jnp.square(xf - mu), -1, keepdims=True)
    y = (xf - mu) * lax.rsqrt(var + LN_EPS) * g.astype(jnp.float32) + b.astype(jnp.float32)
    return y.astype(x.dtype)


def _rms_heads(o, w):
    return o * lax.rsqrt(jnp.mean(o * o, -1, keepdims=True) + RMS_EPS) * w.astype(jnp.float32)


def _l2norm(x):
    return x * lax.rsqrt(jnp.sum(x * x, -1, keepdims=True) + L2_EPS)


def _causal_conv(u, buf, w):
    L = u.shape[1]
    up = jnp.concatenate([buf, u], axis=1)
    y = up[:, 0:L] * w[0]
    for j in range(1, CONV_W):
        y = y + up[:, j:j + L] * w[j]
    return jax.nn.silu(y), up[:, -(CONV_W - 1):]


def _to_blocks(t, c):
    B, L = t.shape[:2]
    n = -(-L // c)
    t = jnp.pad(t, [(0, 0), (0, n * c - L)] + [(0, 0)] * (t.ndim - 2))
    t = t.reshape((B, n, c) + t.shape[2:])
    return jnp.swapaxes(jnp.moveaxis(t, 3, 2), 0, 1)


def _from_blocks(o, L):
    o = jnp.moveaxis(jnp.swapaxes(o, 0, 1), 2, 3)
    B, n, c = o.shape[:3]
    return o.reshape((B, n * c) + o.shape[3:])[:, :L]


def _gated_delta_chunked(q, k, v, g, beta, s0):
    L = q.shape[1]
    c = CHUNK
    qb, kb, vb = _to_blocks(q, c), _to_blocks(k, c), _to_blocks(v, c)
    gb, bb = _to_blocks(g, c), _to_blocks(beta, c)
    incl = jnp.tril(jnp.ones((c, c), dtype=bool))
    strict = jnp.tril(jnp.ones((c, c), dtype=bool), -1)
    gc = jnp.cumsum(gb, axis=-1)
    diff = gc[..., :, None] - gc[..., None, :]
    decay = jnp.where(incl, jnp.exp(jnp.where(incl, diff, 0.0)), 0.0)
    k_beta = kb * bb[..., None]
    v_beta = vb * bb[..., None]
    lmat = jnp.where(strict, jnp.einsum('nbhtk,nbhsk->nbhts', k_beta, kb) * decay, 0.0)
    rhs = jnp.concatenate([v_beta, k_beta * jnp.exp(gc)[..., None]], axis=-1)
    sol = lax.linalg.triangular_solve(lmat, rhs, left_side=True, lower=True, unit_diagonal=True)
    value, k_cum = sol[..., :DV_A], sol[..., DV_A:]
    a_qk = jnp.einsum('nbhtk,nbhsk->nbhts', qb, kb) * decay
    q_dec = qb * jnp.exp(gc)[..., None]
    k_dec = kb * jnp.exp(gc[..., -1:] - gc)[..., None]
    last_dec = jnp.exp(gc[..., -1])

    def step(s, xs):
        qi, ki, vi, kci, ai, ld = xs
        v_new = vi - jnp.einsum('bhtk,bhkv->bhtv', kci, s)
        o = jnp.einsum('bhtk,bhkv->bhtv', qi, s) + jnp.einsum('bhts,bhsv->bhtv', ai, v_new)
        s = s * ld[..., None, None] + jnp.einsum('bhtk,bhtv->bhkv', ki, v_new)
        return s, o

    s_final, o = lax.scan(step, s0, (q_dec, k_dec, value, k_cum, a_qk, last_dec))
    return _from_blocks(o, L), s_final


def _hgrn2_chunked(q, k, v, logf, s0):
    L = q.shape[1]
    c = HGRN_BLOCK
    qb, kb, vb, fb = _to_blocks(q, c), _to_blocks(k, c), _to_blocks(v, c), _to_blocks(logf, c)
    incl = jnp.tril(jnp.ones((c, c), dtype=bool))[:, :, None]

    def step(s, xs):
        qi, ki, vi, lfi = xs
        bcum = jnp.cumsum(lfi, axis=2)
        blast = bcum[:, :, -1]
        diff = bcum[:, :, :, None, :] - bcum[:, :, None, :, :]
        dec = jnp.where(incl, jnp.exp(jnp.where(incl, diff, 0.0)), 0.0)
        a = jnp.einsum('bhtk,bhsk,bhtsk->bhts', qi, ki, dec)
        o = jnp.einsum('bhtk,bhkv->bhtv', qi * jnp.exp(bcum), s) + jnp.einsum('bhts,bhsv->bhtv', a, vi)
        s = s * jnp.exp(blast)[..., None] + jnp.einsum('bhsk,bhsv->bhkv', ki * jnp.exp(blast[:, :, None, :] - bcum), vi)
        return s, o

    s_final, o = lax.scan(step, s0, (qb, kb, vb, fb))
    return _from_blocks(o, L), s_final


def _token_mixers(x, conv_buf, s_gdn, s_hgrn, w_in, conv_w, a_log, dt_bias, gdn_norm_w, lb,
                  hgrn_norm_w, w_br_a, w_br_b, w_out):
    f32 = jnp.float32
    B, L, _ = x.shape
    proj = (x @ w_in).astype(f32)
    qa, ka, va, ga, aa, ba, qh, fh, ih, gh, mga, mgb = jnp.split(proj, _split_points(), axis=-1)
    qkv, new_buf = _causal_conv(jnp.concatenate([qa, ka, va], axis=-1), conv_buf.astype(f32), conv_w.astype(f32))
    qa, ka, va = jnp.split(qkv, [WA_QK, 2 * WA_QK], axis=-1)
    qa = _l2norm(qa.reshape(B, L, N_HEADS_A, DK_A)) * DK_A ** -0.5
    ka = _l2norm(ka.reshape(B, L, N_HEADS_A, DK_A))
    va = va.reshape(B, L, N_HEADS_A, DV_A)
    g = -jnp.exp(a_log.astype(f32)) * jax.nn.softplus(aa + dt_bias.astype(f32))
    beta = jax.nn.sigmoid(ba)
    oa, s_gdn_new = _gated_delta_chunked(qa, ka, va, g, beta, s_gdn.astype(f32))
    oa = _rms_heads(oa, gdn_norm_w) * jax.nn.silu(ga.reshape(B, L, N_HEADS_A, DV_A))
    z = fh.reshape(B, L, N_HEADS_B, DK_B)
    lbf = lb.astype(f32)
    logf = jnp.log(lbf + (1.0 - lbf) * jax.nn.sigmoid(z))
    kh = (1.0 - lbf) * jax.nn.sigmoid(-z)
    qh = jax.nn.silu(qh.reshape(B, L, N_HEADS_B, DK_B)) * DK_B ** -0.5
    vh = ih.reshape(B, L, N_HEADS_B, DV_B)
    ob, s_hgrn_new = _hgrn2_chunked(qh, kh, vh, logf, s_hgrn.astype(f32))
    ob = _rms_heads(ob, hgrn_norm_w) * jax.nn.sigmoid(gh.reshape(B, L, N_HEADS_B, DV_B))
    ya = oa.reshape(B, L, WA_V).astype(x.dtype) @ w_br_a
    yb = ob.reshape(B, L, WB_V).astype(x.dtype) @ w_br_b
    merged = jax.nn.sigmoid(mga).astype(x.dtype) * ya + jax.nn.sigmoid(mgb).astype(x.dtype) * yb
    return (merged @ w_out, new_buf.astype(x.dtype), s_gdn_new.astype(x.dtype), s_hgrn_new.astype(x.dtype))


def _layer(x, conv_buf, s_gdn, s_hgrn, w_in, conv_w, a_log, dt_bias, gdn_norm_w, lb, hgrn_norm_w,
           w_br_a, w_br_b, w_out, ln1_g, ln1_b, w_gate_up, w_down, ln2_g, ln2_b):
    mix, new_buf, s_gdn_new, s_hgrn_new = _token_mixers(x, conv_buf, s_gdn, s_hgrn, w_in, conv_w, a_log, dt_bias,
                                                        gdn_norm_w, lb, hgrn_norm_w, w_br_a, w_br_b, w_out)
    x = _layer_norm(ALPHA * x + mix, ln1_g, ln1_b)
    gate, up = jnp.split(x @ w_gate_up, 2, axis=-1)
    x = _layer_norm(ALPHA * x + (jax.nn.silu(gate) * up) @ w_down, ln2_g, ln2_b)
    return x, new_buf, s_gdn_new, s_hgrn_new


def setup_inputs(seed: int = 0) -> dict:
    key = jax.random.key(seed)
    ks = jax.random.split(key, 24)

    def nrm(k, shape, s):
        return jax.random.normal(k, shape, jnp.float32) * s

    dt = jnp.exp(jax.random.uniform(ks[8], (DEPTH, N_HEADS_A), jnp.float32, math.log(1e-3), math.log(1e-1)))
    return {
        'x_prompt': nrm(ks[0], (BATCH, SEQ, D_MODEL), 1.0),
        'x_sample': nrm(ks[1], (DEC_BATCH, DEC_SEQ, D_MODEL), 1.0),
        'cache_gdn_conv': nrm(ks[2], (DEPTH, DEC_BATCH, CONV_W - 1, CONV_CH), 1.0),
        'state_gdn': nrm(ks[3], (DEPTH, DEC_BATCH, N_HEADS_A, DK_A, DV_A), 0.1),
        'state_hgrn': nrm(ks[4], (DEPTH, DEC_BATCH, N_HEADS_B, DK_B, DV_B), 0.1),
        'w_in': nrm(ks[5], (DEPTH, D_MODEL, IN_TOTAL), D_MODEL ** -0.5),
        'conv_w': nrm(ks[6], (DEPTH, CONV_W, CONV_CH), CONV_W ** -0.5),
        'a_log': jnp.log(jax.random.uniform(ks[7], (DEPTH, N_HEADS_A), jnp.float32, 1.0, 16.0)),
        'dt_bias': dt + jnp.log(-jnp.expm1(-dt)),
        'gdn_norm_w': 1.0 + nrm(ks[9], (DEPTH, DV_A), 0.02),
        'hgrn_lb_logits': nrm(ks[10], (DEPTH + 1, WB_K), 0.5),
        'hgrn_norm_w': 1.0 + nrm(ks[11], (DEPTH, DV_B), 0.02),
        'w_br_a': nrm(ks[12], (DEPTH, WA_V, D_MODEL), WA_V ** -0.5),
        'w_br_b': nrm(ks[13], (DEPTH, WB_V, D_MODEL), WB_V ** -0.5),
        'w_out': nrm(ks[14], (DEPTH, D_MODEL, D_MODEL), BETA_INIT * D_MODEL ** -0.5),
        'ln1_g': 1.0 + nrm(ks[15], (DEPTH, D_MODEL), 0.02),
        'ln1_b': nrm(ks[16], (DEPTH, D_MODEL), 0.02),
        'w_gate_up': nrm(ks[17], (DEPTH, D_MODEL, 2 * D_FF), D_MODEL ** -0.5),
        'w_down': nrm(ks[18], (DEPTH, D_FF, D_MODEL), BETA_INIT * D_FF ** -0.5),
        'ln2_g': 1.0 + nrm(ks[19], (DEPTH, D_MODEL), 0.02),
        'ln2_b': nrm(ks[20], (DEPTH, D_MODEL), 0.02),
    }


def reference(x_prompt, x_sample, cache_gdn_conv, state_gdn, state_hgrn, w_in, conv_w, a_log, dt_bias,
              gdn_norm_w, hgrn_lb_logits, hgrn_norm_w, w_br_a, w_br_b, w_out, ln1_g, ln1_b, w_gate_up,
              w_down, ln2_g, ln2_b):
    B = x_prompt.shape[0]
    dt = x_prompt.dtype
    lb_all = jnp.cumsum(jax.nn.softmax(hgrn_lb_logits.astype(jnp.float32), axis=0), axis=0)
    zero_conv = jnp.zeros((B, CONV_W - 1, CONV_CH), dt)
    zero_gdn = jnp.zeros((B, N_HEADS_A, DK_A, DV_A), dt)
    zero_hgrn = jnp.zeros((B, N_HEADS_B, DK_B, DV_B), dt)
    y_prompt, y_sample = x_prompt, x_sample
    conv_p, gdn_p, hgrn_p, conv_s, gdn_s, hgrn_s = [], [], [], [], [], []
    for l in range(DEPTH):
        wl = (w_in[l], conv_w[l], a_log[l], dt_bias[l], gdn_norm_w[l], lb_all[l].reshape(N_HEADS_B, DK_B),
              hgrn_norm_w[l], w_br_a[l], w_br_b[l], w_out[l], ln1_g[l], ln1_b[l], w_gate_up[l], w_down[l],
              ln2_g[l], ln2_b[l])
        y_prompt, cp, gp, hp = _layer(y_prompt, zero_conv, zero_gdn, zero_hgrn, *wl)
        y_sample, cs, gs, hs = _layer(y_sample, cache_gdn_conv[l], state_gdn[l], state_hgrn[l], *wl)
        conv_p.append(cp)
        gdn_p.append(gp)
        hgrn_p.append(hp)
        conv_s.append(cs)
        gdn_s.append(gs)
        hgrn_s.append(hs)
    return (y_prompt, y_sample, jnp.stack(conv_p), jnp.stack(gdn_p), jnp.stack(hgrn_p),
            jnp.stack(conv_s), jnp.stack(gdn_s), jnp.stack(hgrn_s))
```

```python
import functools
import math

import jax
import jax.numpy as jnp
from jax import lax
from jax.experimental import pallas as pl
from jax.experimental.pallas import tpu as pltpu

F32 = jnp.float32
BF16 = jnp.bfloat16

D_MODEL = 1024
N_HEADS = 8
D_HEAD = 128
CONV_W = 4
CONV_CH = 3 * D_MODEL
D_FF = 2816
ALPHA = 2.0 ** 0.25
LN_EPS = 1e-5
RMS_EPS = 1e-6
L2_EPS = 1e-6
EXP_CLAMP = 80.0

P_QKV = 0
P_REST = CONV_CH
P_MG = P_REST + 5 * D_MODEL
P_AB = P_MG + 2 * D_MODEL
P_TOTAL = P_AB + 128
R_GA, R_QH, R_FH, R_IH, R_GH, R_AB = 0, 1024, 2048, 3072, 4096, 5120
R_TOTAL = R_AB + 128

VMEM_LIMIT_BYTES = 60 * 1024 * 1024


def _dot(a, b):
    return jnp.dot(a, b, preferred_element_type=F32)


def _dot_nt(a, b):
    return lax.dot_general(a, b, (((1,), (1,)), ((), ())), preferred_element_type=F32)


def _dot_tn(a, b):
    return lax.dot_general(a, b, (((0,), (0,)), ((), ())), preferred_element_type=F32)


def _sigmoid(x):
    return 1.0 / (1.0 + jnp.exp(-x))


def _split_bf16(x):
    hi = x.astype(BF16)
    lo = (x - hi.astype(F32)).astype(BF16)
    return hi, lo


def _layer_norm(x, g, b):
    mu = jnp.mean(x, axis=-1, keepdims=True)
    xc = x - mu
    var = jnp.mean(xc * xc, axis=-1, keepdims=True)
    return xc * lax.rsqrt(var + LN_EPS) * g + b


def _mixer_kernel(layer, T, CG, CH, nt,
                  x_ref, cache_ref, sg0_ref, sh0_ref, w_ref, cw_ref, alog_ref, dtb_ref, gnw_ref,
                  lbl_ref, hnw_ref,
                  oa_ref, ob_ref, mg_ref, conv_ref, sg_ref, sh_ref,
                  cb, qkv, pr, bsc, sht):
    i = pl.program_id(1)

    @pl.when(i == 0)
    def _init():
        cb[0:8, :] = cache_ref[...]
        sg_ref[...] = sg0_ref[...]
        for h in range(N_HEADS):
            sht[h] = sh0_ref[h].T

    xb = x_ref[...].astype(BF16)
    for s in range(3):
        cb[8:8 + T, s * 1024:(s + 1) * 1024] = _dot(xb, w_ref[:, P_QKV + s * 1024:P_QKV + (s + 1) * 1024])
    for s in range(5):
        pr[:, s * 1024:(s + 1) * 1024] = _dot(xb, w_ref[:, P_REST + s * 1024:P_REST + (s + 1) * 1024])
    for s in range(2):
        mg_ref[:, s * 1024:(s + 1) * 1024] = _dot(xb, w_ref[:, P_MG + s * 1024:P_MG + (s + 1) * 1024]).astype(BF16)
    pr[:, R_AB:R_AB + 128] = _dot(xb, w_ref[:, P_AB:P_AB + 128])

    for blk in range(CONV_CH // 128):
        cs = slice(blk * 128, (blk + 1) * 128)
        y = (cb[5:5 + T, cs] * cw_ref[0:1, cs] + cb[6:6 + T, cs] * cw_ref[1:2, cs]
             + cb[7:7 + T, cs] * cw_ref[2:3, cs] + cb[8:8 + T, cs] * cw_ref[3:4, cs])
        y = y * _sigmoid(y)
        if blk < 2 * N_HEADS:
            y = y * lax.rsqrt(jnp.sum(y * y, axis=-1, keepdims=True) + L2_EPS)
            if blk < N_HEADS:
                y = y * (D_HEAD ** -0.5)
        qkv[:, cs] = y

    @pl.when(i == nt - 1)
    def _conv_state():
        conv_ref[...] = cb[T + 5:T + 8, :]

    cb[0:8, :] = cb[T:T + 8, :]

    neg_a = -jnp.exp(alog_ref[...])
    dtb = dtb_ref[...]
    gnw = gnw_ref[...]
    lane_g = lax.broadcasted_iota(jnp.int32, (CG, 128), 1)
    row_g = lax.broadcasted_iota(jnp.int32, (CG, CG), 0)
    col_g = lax.broadcasted_iota(jnp.int32, (CG, CG), 1)
    incl_g = row_g >= col_g
    strict_g = row_g > col_g
    tril_g = incl_g.astype(BF16)
    n_neumann = int(math.log2(CG)) - 1

    def gdn_chunk(c, carry):
        r0 = pl.multiple_of(c * CG, CG)
        rows = pl.ds(r0, CG)
        ab = pr[rows, R_AB:R_AB + 128]
        xa = ab + dtb
        softplus = jnp.maximum(xa, 0.0) + jnp.log(1.0 + jnp.exp(-jnp.abs(xa)))
        g = jnp.where(lane_g < N_HEADS, neg_a * softplus, 0.0)
        beta = _sigmoid(ab)
        g_hi, g_lo = _split_bf16(g)
        gc = _dot(tril_g, g_hi) + _dot(tril_g, g_lo)
        if CG < 128:
            gc_sq = jnp.concatenate([gc, jnp.zeros((128 - CG, 128), F32)], axis=0)
        else:
            gc_sq = gc
        gc_t = gc_sq.T
        egc = jnp.exp(gc)
        g_last = gc[CG - 1:CG, :]
        e_to_end = jnp.exp(g_last - gc)
        e_last = jnp.exp(g_last)
        for h in range(N_HEADS):
            cq = slice(h * 128, (h + 1) * 128)
            q = qkv[rows, h * 128:(h + 1) * 128]
            k = qkv[rows, 1024 + h * 128:1024 + (h + 1) * 128]
            v = qkv[rows, 2048 + h * 128:2048 + (h + 1) * 128]
            g_col = gc[:, h:h + 1]
            g_row = gc_t[h:h + 1, 0:CG]
            b_col = beta[:, N_HEADS + h:N_HEADS + h + 1]
            eg_col = egc[:, h:h + 1]
            dec = jnp.exp(jnp.minimum(g_col - g_row, 0.0))
            kb = k * b_col
            vb = v * b_col
            m = _dot_nt(jnp.concatenate([kb, q], axis=0).astype(BF16), k.astype(BF16))
            lm = jnp.where(strict_g, m[0:CG] * dec, 0.0)
            a_qk = jnp.where(incl_g, m[CG:2 * CG] * dec, 0.0)
            p = -lm
            lp = lm
            for _ in range(n_neumann):
                lpb = lp.astype(BF16)
                lp = _dot(lpb, lpb)
                p = p + lp + _dot(p.astype(BF16), lp.astype(BF16))
            rhs = jnp.concatenate([vb, kb * eg_col], axis=1)
            sol = rhs + _dot(p.astype(BF16), rhs.astype(BF16))
            value = sol[:, 0:128]
            k_cum = sol[:, 128:256]
            q_dec = q * eg_col
            k_dec = k * e_to_end[:, h:h + 1]
            s = sg_ref[h]
            qs = _dot(jnp.concatenate([k_cum, q_dec], axis=0).astype(BF16), s.astype(BF16))
            v_new = value - qs[0:CG]
            o = qs[CG:2 * CG] + _dot(a_qk.astype(BF16), v_new.astype(BF16))
            sg_ref[h] = s * e_last[:, h:h + 1] + _dot_tn(k_dec.astype(BF16), v_new.astype(BF16))
            ms = jnp.mean(o * o, axis=-1, keepdims=True)
            gate = pr[rows, R_GA + h * 128:R_GA + (h + 1) * 128]
            oa_ref[rows, cq] = (o * lax.rsqrt(ms + RMS_EPS) * gnw * (gate * _sigmoid(gate))).astype(BF16)
        return carry

    lax.fori_loop(0, T // CG, gdn_chunk, 0)

    ll = lbl_ref[...]
    el = jnp.exp(ll - jnp.max(ll, axis=0, keepdims=True))
    lb = jnp.sum(el[0:layer + 1], axis=0, keepdims=True) / jnp.sum(el, axis=0, keepdims=True)
    hnw = hnw_ref[...]
    sh = int(math.log2(CH))
    row_t = lax.broadcasted_iota(jnp.int32, (T, T), 0)
    col_t = lax.broadcasted_iota(jnp.int32, (T, T), 1)
    tril_t = (((row_t >> sh) == (col_t >> sh)) & (row_t >= col_t)).astype(BF16)
    for h in range(N_HEADS):
        cs = slice(h * 128, (h + 1) * 128)
        lbh = lb[:, cs]
        z = pr[:, R_FH + h * 128:R_FH + (h + 1) * 128]
        e = jnp.exp(-jnp.abs(z))
        r = 1.0 / (1.0 + e)
        sig_pos = jnp.where(z >= 0.0, r, e * r)
        sig_neg = jnp.where(z >= 0.0, e * r, r)
        logf = jnp.log(lbh + (1.0 - lbh) * sig_pos)
        qh = pr[:, R_QH + h * 128:R_QH + (h + 1) * 128]
        pr[:, R_QH + h * 128:R_QH + (h + 1) * 128] = qh * _sigmoid(qh) * (D_HEAD ** -0.5)
        pr[:, R_FH + h * 128:R_FH + (h + 1) * 128] = (1.0 - lbh) * sig_neg
        l_hi, l_lo = _split_bf16(logf)
        bsc[:, cs] = _dot(tril_t, l_hi) + _dot(tril_t, l_lo)

    row_c = lax.broadcasted_iota(jnp.int32, (CH, CH), 0)
    col_c = lax.broadcasted_iota(jnp.int32, (CH, CH), 1)
    incl_c = row_c >= col_c

    def hgrn_chunk(c, carry):
        r0 = pl.multiple_of(c * CH, CH)
        rows = pl.ds(r0, CH)
        for h in range(N_HEADS):
            cs = slice(h * 128, (h + 1) * 128)
            q = pr[rows, R_QH + h * 128:R_QH + (h + 1) * 128]
            k = pr[rows, R_FH + h * 128:R_FH + (h + 1) * 128]
            v = pr[rows, R_IH + h * 128:R_IH + (h + 1) * 128]
            b = bsc[rows, cs]
            b_mid = b[CH // 2 - 1:CH // 2, :]
            b_last = b[CH - 1:CH, :]
            bm = b - b_mid
            q_in = (q * jnp.exp(jnp.minimum(bm, EXP_CLAMP))).astype(BF16)
            k_in = (k * jnp.exp(jnp.minimum(-bm, EXP_CLAMP))).astype(BF16)
            a = jnp.where(incl_c, _dot_nt(q_in, k_in), 0.0)
            st = sht[h]
            o = _dot_nt((q * jnp.exp(b)).astype(BF16), st.astype(BF16)) + _dot(a.astype(BF16), v.astype(BF16))
            k_end = (k * jnp.exp(b_last - b)).astype(BF16)
            sht[h] = st * jnp.exp(b_last) + _dot_tn(v.astype(BF16), k_end)
            ms = jnp.mean(o * o, axis=-1, keepdims=True)
            gate = pr[rows, R_GH + h * 128:R_GH + (h + 1) * 128]
            ob_ref[rows, cs] = (o * lax.rsqrt(ms + RMS_EPS) * hnw * _sigmoid(gate)).astype(BF16)
        return carry

    lax.fori_loop(0, T // CH, hgrn_chunk, 0)

    @pl.when(i == nt - 1)
    def _hgrn_state():
        for h in range(N_HEADS):
            sh_ref[h] = sht[h].T


def _ffn_kernel(FC, x_ref, oa_ref, ob_ref, mg_ref, wa_ref, wb_ref, wo_ref, g1_ref, b1_ref,
                wgu_ref, wd_ref, g2_ref, b2_ref, y_ref):
    ya = _dot(oa_ref[...], wa_ref[...])
    yb = _dot(ob_ref[...], wb_ref[...])
    merged = (_sigmoid(mg_ref[:, 0:D_MODEL].astype(F32)) * ya
              + _sigmoid(mg_ref[:, D_MODEL:2 * D_MODEL].astype(F32)) * yb)
    mix = _dot(merged.astype(BF16), wo_ref[...])
    x1 = _layer_norm(ALPHA * x_ref[...] + mix, g1_ref[...], b1_ref[...])
    x1b = x1.astype(BF16)
    acc = jnp.zeros(x1.shape, F32)
    for j in range(D_FF // FC):
        gate = _dot(x1b, wgu_ref[:, j * FC:(j + 1) * FC])
        up = _dot(x1b, wgu_ref[:, D_FF + j * FC:D_FF + (j + 1) * FC])
        act = (gate * _sigmoid(gate) * up).astype(BF16)
        acc = acc + _dot(act, wd_ref[j * FC:(j + 1) * FC, :])
    y_ref[...] = _layer_norm(ALPHA * x1 + acc, g2_ref[...], b2_ref[...])


def _const_spec(shape):
    nd = len(shape)
    return pl.BlockSpec(shape, lambda *_: (0,) * nd, pipeline_mode=pl.Buffered(1))


def _tiles(L):
    T = min(256, L)
    assert L % T == 0 and L >= CONV_W - 1
    CG = min(64, T)
    CH = min(16, T)
    assert T % CG == 0 and T % CH == 0 and CG & (CG - 1) == 0 and CH & (CH - 1) == 0
    return T, CG, CH


def _layer(layer, x, conv_buf, s_gdn, s_hgrn, w_in_p, conv_w, a_log, dt_bias, gdn_norm_w, lb_logits,
           hgrn_norm_w, w_br_a, w_br_b, w_out, ln1_g, ln1_b, w_gate_up, w_down, ln2_g, ln2_b):
    B, L, _ = x.shape
    T, CG, CH = _tiles(L)
    nt = L // T
    cache8 = jnp.pad(conv_buf.astype(F32), ((0, 0), (8 - (CONV_W - 1), 0), (0, 0)))
    row128 = lambda v: jnp.pad(v.astype(F32).reshape(1, -1), ((0, 0), (0, 128 - v.shape[-1])))

    seq = lambda w: pl.BlockSpec((None, T, w), lambda b, i: (b, i, 0))
    per_b = lambda *s: pl.BlockSpec((None,) + s, lambda b, i: (b,) + (0,) * len(s))
    oa, ob, mg, conv_new, sg_new, sh_new = pl.pallas_call(
        functools.partial(_mixer_kernel, layer, T, CG, CH, nt),
        grid=(B, nt),
        in_specs=[seq(D_MODEL), per_b(8, CONV_CH), per_b(N_HEADS, D_HEAD, D_HEAD), per_b(N_HEADS, D_HEAD, D_HEAD),
                  _const_spec((D_MODEL, P_TOTAL)), _const_spec((CONV_W, CONV_CH)), _const_spec((1, 128)),
                  _const_spec((1, 128)), _const_spec((1, 128)), _const_spec(lb_logits.shape),
                  _const_spec((1, 128))],
        out_specs=[seq(D_MODEL), seq(D_MODEL), seq(2 * D_MODEL), per_b(CONV_W - 1, CONV_CH),
                   per_b(N_HEADS, D_HEAD, D_HEAD), per_b(N_HEADS, D_HEAD, D_HEAD)],
        out_shape=[jax.ShapeDtypeStruct((B, L, D_MODEL), BF16), jax.ShapeDtypeStruct((B, L, D_MODEL), BF16),
                   jax.ShapeDtypeStruct((B, L, 2 * D_MODEL), BF16),
                   jax.ShapeDtypeStruct((B, CONV_W - 1, CONV_CH), F32),
                   jax.ShapeDtypeStruct((B, N_HEADS, D_HEAD, D_HEAD), F32),
                   jax.ShapeDtypeStruct((B, N_HEADS, D_HEAD, D_HEAD), F32)],
        scratch_shapes=[pltpu.VMEM((T + 8, CONV_CH), F32), pltpu.VMEM((T, CONV_CH), F32),
                        pltpu.VMEM((T, R_TOTAL), F32), pltpu.VMEM((T, D_MODEL), F32),
                        pltpu.VMEM((N_HEADS, D_HEAD, D_HEAD), F32)],
        compiler_params=pltpu.CompilerParams(dimension_semantics=("parallel", "arbitrary"),
                                             vmem_limit_bytes=VMEM_LIMIT_BYTES),
        name="mixer",
    )(x, cache8, s_gdn.astype(F32), s_hgrn.astype(F32), w_in_p, conv_w.astype(F32), row128(a_log),
      row128(dt_bias), gdn_norm_w.astype(F32).reshape(1, -1), lb_logits.astype(F32),
      hgrn_norm_w.astype(F32).reshape(1, -1))

    N = B * L
    T2 = min(256, N)
    assert N % T2 == 0
    FC = 1408
    tok = lambda w: pl.BlockSpec((T2, w), lambda i: (i, 0))
    row = lambda v: v.astype(F32).reshape(1, -1)
    y = pl.pallas_call(
        functools.partial(_ffn_kernel, FC),
        grid=(N // T2,),
        in_specs=[tok(D_MODEL), tok(D_MODEL), tok(D_MODEL), tok(2 * D_MODEL),
                  _const_spec((D_MODEL, D_MODEL)), _const_spec((D_MODEL, D_MODEL)), _const_spec((D_MODEL, D_MODEL)),
                  _const_spec((1, D_MODEL)), _const_spec((1, D_MODEL)),
                  _const_spec((D_MODEL, 2 * D_FF)), _const_spec((D_FF, D_MODEL)),
                  _const_spec((1, D_MODEL)), _const_spec((1, D_MODEL))],
        out_specs=tok(D_MODEL),
        out_shape=jax.ShapeDtypeStruct((N, D_MODEL), F32),
        compiler_params=pltpu.CompilerParams(dimension_semantics=("parallel",),
                                             vmem_limit_bytes=VMEM_LIMIT_BYTES),
        name="ffn",
    )(x.reshape(N, D_MODEL), oa.reshape(N, D_MODEL), ob.reshape(N, D_MODEL), mg.reshape(N, 2 * D_MODEL),
      w_br_a, w_br_b, w_out, row(ln1_g), row(ln1_b), w_gate_up, w_down, row(ln2_g), row(ln2_b))
    return y.reshape(B, L, D_MODEL), conv_new, sg_new, sh_new


def _permute_w_in(w):
    n_ab = 2 * N_HEADS
    head = w[:, 0:4 * D_MODEL]
    ab = w[:, 4 * D_MODEL:4 * D_MODEL + n_ab]
    tail = w[:, 4 * D_MODEL + n_ab:]
    pad = jnp.zeros((w.shape[0], 128 - n_ab), w.dtype)
    return jnp.concatenate([head, tail, ab, pad], axis=1).astype(BF16)


def kernel(x_prompt, x_sample, cache_gdn_conv, state_gdn, state_hgrn, w_in, conv_w, a_log, dt_bias, gdn_norm_w, hgrn_lb_logits, hgrn_norm_w, w_br_a, w_br_b, w_out, ln1_g, ln1_b, w_gate_up, w_down, ln2_g, ln2_b):
    depth = w_in.shape[0]
    assert depth == 1, "ALPHA is baked for a single layer"
    B = x_prompt.shape[0]
    dt = x_prompt.dtype
    y_p, y_s = x_prompt, x_sample
    outs = [[] for _ in range(6)]
    for l in range(depth):
        wl = (_permute_w_in(w_in[l]), conv_w[l], a_log[l], dt_bias[l], gdn_norm_w[l], hgrn_lb_logits,
              hgrn_norm_w[l], w_br_a[l].astype(BF16), w_br_b[l].astype(BF16), w_out[l].astype(BF16),
              ln1_g[l], ln1_b[l], w_gate_up[l].astype(BF16), w_down[l].astype(BF16), ln2_g[l], ln2_b[l])
        zc = jnp.zeros((B, CONV_W - 1, CONV_CH), F32)
        zs = jnp.zeros((B, N_HEADS, D_HEAD, D_HEAD), F32)
        y_p, cp, gp, hp = _layer(l, y_p, zc, zs, zs, *wl)
        y_s, cs, gs, hs = _layer(l, y_s, cache_gdn_conv[l], state_gdn[l], state_hgrn[l], *wl)
        for lst, v in zip(outs, (cp, gp, hp, cs, gs, hs)):
            lst.append(v.astype(dt))
    return (y_p.astype(dt), y_s.astype(dt)) + tuple(jnp.stack(o) for o in outs)
```

```python
import functools
import math

import jax
import jax.numpy as jnp
from jax import lax
from jax.experimental import pallas as pl
from jax.experimental.pallas import tpu as pltpu

F32 = jnp.float32
BF16 = jnp.bfloat16

D_MODEL = 1024
N_HEADS = 8
D_HEAD = 128
CONV_W = 4
CONV_CH = 3 * D_MODEL
D_FF = 2816
ALPHA = 2.0 ** 0.25
LN_EPS = 1e-5
RMS_EPS = 1e-6
L2_EPS = 1e-6
EXP_CLAMP = 80.0

P_QKV = 0
P_REST = CONV_CH
P_MG = P_REST + 5 * D_MODEL
P_AB = P_MG + 2 * D_MODEL
P_TOTAL = P_AB + 128
R_GA, R_QH, R_FH, R_IH, R_GH, R_AB = 0, 1024, 2048, 3072, 4096, 5120
R_TOTAL = R_AB + 128
Q_OFF, K_OFF, V_OFF = 0, 1024, 2048

VMEM_LIMIT_BYTES = 60 * 1024 * 1024


def _dot(a, b):
    return jnp.dot(a, b, preferred_element_type=F32)


def _dot_nt(a, b):
    return lax.dot_general(a, b, (((1,), (1,)), ((), ())), preferred_element_type=F32)


def _dot_tn(a, b):
    return lax.dot_general(a, b, (((0,), (0,)), ((), ())), preferred_element_type=F32)


def _sigmoid(x):
    return 1.0 / (1.0 + jnp.exp(-x))


def _split_bf16(x):
    hi = x.astype(BF16)
    lo = (x - hi.astype(F32)).astype(BF16)
    return hi, lo


def _layer_norm(x, g, b):
    mu = jnp.mean(x, axis=-1, keepdims=True)
    xc = x - mu
    var = jnp.mean(xc * xc, axis=-1, keepdims=True)
    return xc * lax.rsqrt(var + LN_EPS) * g + b


def _hs(off, h):
    return slice(off + h * D_HEAD, off + (h + 1) * D_HEAD)


def _mixer_kernel(layer, T, CG, CH, nt,
                  x_ref, cache_ref, sg0_ref, sh0_ref, w_ref, cw_ref, alog_ref, dtb_ref, gnw_ref,
                  lbl_ref, hnw_ref,
                  oa_ref, ob_ref, mg_ref, conv_ref, sg_ref, sh_ref,
                  cb, qkv, pr, bsc, kcs, aqs, els, sht):
    i = pl.program_id(1)
    heads = range(N_HEADS)

    @pl.when(i == 0)
    def _init():
        cb[0:8, :] = cache_ref[...]
        sg_ref[...] = sg0_ref[...]
        for h in heads:
            sht[h] = sh0_ref[h].T

    xb = x_ref[...].astype(BF16)
    for s in range(3):
        cb[8:8 + T, s * 1024:(s + 1) * 1024] = _dot(xb, w_ref[:, P_QKV + s * 1024:P_QKV + (s + 1) * 1024])
    for s in range(5):
        pr[:, s * 1024:(s + 1) * 1024] = _dot(xb, w_ref[:, P_REST + s * 1024:P_REST + (s + 1) * 1024])
    for s in range(2):
        mg_ref[:, s * 1024:(s + 1) * 1024] = _dot(xb, w_ref[:, P_MG + s * 1024:P_MG + (s + 1) * 1024]).astype(BF16)
    pr[:, R_AB:R_AB + 128] = _dot(xb, w_ref[:, P_AB:P_AB + 128])

    for blk in range(CONV_CH // 128):
        cs = slice(blk * 128, (blk + 1) * 128)
        y = (cb[5:5 + T, cs] * cw_ref[0:1, cs] + cb[6:6 + T, cs] * cw_ref[1:2, cs]
             + cb[7:7 + T, cs] * cw_ref[2:3, cs] + cb[8:8 + T, cs] * cw_ref[3:4, cs])
        y = y * _sigmoid(y)
        if blk < 2 * N_HEADS:
            y = y * lax.rsqrt(jnp.sum(y * y, axis=-1, keepdims=True) + L2_EPS)
            if blk < N_HEADS:
                y = y * (D_HEAD ** -0.5)
        qkv[:, cs] = y

    @pl.when(i == nt - 1)
    def _conv_state():
        conv_ref[...] = cb[T + 5:T + 8, :]

    cb[0:8, :] = cb[T:T + 8, :]

    neg_a = -jnp.exp(alog_ref[...])
    dtb = dtb_ref[...]
    gnw = gnw_ref[...]
    lane_g = lax.broadcasted_iota(jnp.int32, (CG, 128), 1)
    row_g = lax.broadcasted_iota(jnp.int32, (CG, CG), 0)
    col_g = lax.broadcasted_iota(jnp.int32, (CG, CG), 1)
    incl_g = row_g >= col_g
    strict_g = row_g > col_g
    tril_g = incl_g.astype(BF16)
    n_neumann = int(math.log2(CG)) - 1

    def gdn_prepare(c, carry):
        r0 = pl.multiple_of(c * CG, CG)
        rows = pl.ds(r0, CG)
        ab = pr[rows, R_AB:R_AB + 128]
        xa = ab + dtb
        softplus = jnp.maximum(xa, 0.0) + jnp.log(1.0 + jnp.exp(-jnp.abs(xa)))
        g = jnp.where(lane_g < N_HEADS, neg_a * softplus, 0.0)
        beta = _sigmoid(ab)
        g_hi, g_lo = _split_bf16(g)
        gc = _dot(tril_g, g_hi) + _dot(tril_g, g_lo)
        if CG < 128:
            gc_sq = jnp.concatenate([gc, jnp.zeros((128 - CG, 128), F32)], axis=0)
        else:
            gc_sq = gc
        gc_t = gc_sq.T
        egc = jnp.exp(gc)
        g_last = gc[CG - 1:CG, :]
        e_to_end = jnp.exp(g_last - gc)
        els[pl.ds(pl.multiple_of(c * 8, 8), 8), :] = jnp.broadcast_to(jnp.exp(g_last), (8, 128))

        q = [qkv[rows, _hs(Q_OFF, h)] for h in heads]
        k = [qkv[rows, _hs(K_OFF, h)] for h in heads]
        v = [qkv[rows, _hs(V_OFF, h)] for h in heads]
        b_col = [beta[:, N_HEADS + h:N_HEADS + h + 1] for h in heads]
        eg_col = [egc[:, h:h + 1] for h in heads]
        dec = [jnp.exp(jnp.minimum(gc[:, h:h + 1] - gc_t[h:h + 1, 0:CG], 0.0)) for h in heads]
        kb = [k[h] * b_col[h] for h in heads]
        m = [_dot_nt(jnp.concatenate([kb[h], q[h]], axis=0).astype(BF16), k[h].astype(BF16)) for h in heads]
        lp = [jnp.where(strict_g, m[h][0:CG] * dec[h], 0.0) for h in heads]
        for h in heads:
            aqs[rows, h * D_HEAD:h * D_HEAD + CG] = jnp.where(incl_g, m[h][CG:2 * CG] * dec[h], 0.0)
        p = [-lp[h] for h in heads]
        lpb = [lp[h].astype(BF16) for h in heads]
        for step in range(n_neumann):
            lp = [_dot(lpb[h], lpb[h]) for h in heads]
            lpb = [lp[h].astype(BF16) for h in heads]
            p = [p[h] + lp[h] + _dot(p[h].astype(BF16), lpb[h]) for h in heads]
        rhs = [jnp.concatenate([v[h] * b_col[h], kb[h] * eg_col[h]], axis=1) for h in heads]
        sol = [rhs[h] + _dot(p[h].astype(BF16), rhs[h].astype(BF16)) for h in heads]
        for h in heads:
            qkv[rows, _hs(V_OFF, h)] = sol[h][:, 0:D_HEAD]
            kcs[rows, _hs(0, h)] = sol[h][:, D_HEAD:2 * D_HEAD]
            qkv[rows, _hs(Q_OFF, h)] = q[h] * eg_col[h]
            qkv[rows, _hs(K_OFF, h)] = k[h] * e_to_end[:, h:h + 1]
        return carry

    lax.fori_loop(0, T // CG, gdn_prepare, 0)

    def gdn_recur(c, carry):
        r0 = pl.multiple_of(c * CG, CG)
        rows = pl.ds(r0, CG)
        e_last = els[pl.ds(pl.multiple_of(c * 8, 8), 8), :][0:1, :]
        s = [sg_ref[h] for h in heads]
        lhs = [jnp.concatenate([kcs[rows, _hs(0, h)], qkv[rows, _hs(Q_OFF, h)]], axis=0).astype(BF16) for h in heads]
        qs = [_dot(lhs[h], s[h].astype(BF16)) for h in heads]
        v_new = [(qkv[rows, _hs(V_OFF, h)] - qs[h][0:CG]).astype(BF16) for h in heads]
        o = [qs[h][CG:2 * CG] + _dot(aqs[rows, h * D_HEAD:h * D_HEAD + CG].astype(BF16), v_new[h]) for h in heads]
        upd = [_dot_tn(qkv[rows, _hs(K_OFF, h)].astype(BF16), v_new[h]) for h in heads]
        for h in heads:
            sg_ref[h] = s[h] * e_last[:, h:h + 1] + upd[h]
            ms = jnp.mean(o[h] * o[h], axis=-1, keepdims=True)
            gate = pr[rows, _hs(R_GA, h)]
            oa_ref[rows, _hs(0, h)] = (o[h] * lax.rsqrt(ms + RMS_EPS) * gnw * (gate * _sigmoid(gate))).astype(BF16)
        return carry

    lax.fori_loop(0, T // CG, gdn_recur, 0)

    ll = lbl_ref[...]
    el = jnp.exp(ll - jnp.max(ll, axis=0, keepdims=True))
    lb = jnp.sum(el[0:layer + 1], axis=0, keepdims=True) / jnp.sum(el, axis=0, keepdims=True)
    hnw = hnw_ref[...]
    sh = int(math.log2(CH))
    nch = T // CH
    row_t = lax.broadcasted_iota(jnp.int32, (T, T), 0)
    col_t = lax.broadcasted_iota(jnp.int32, (T, T), 1)
    causal_t = ((row_t >> sh) == (col_t >> sh)) & (row_t >= col_t)
    tril_t = causal_t.astype(BF16)
    for h in heads:
        lbh = lb[:, _hs(0, h)]
        z = pr[:, _hs(R_FH, h)]
        e = jnp.exp(-jnp.abs(z))
        r = 1.0 / (1.0 + e)
        sig_pos = jnp.where(z >= 0.0, r, e * r)
        sig_neg = jnp.where(z >= 0.0, e * r, r)
        logf = jnp.log(lbh + (1.0 - lbh) * sig_pos)
        k = (1.0 - lbh) * sig_neg
        qh = pr[:, _hs(R_QH, h)]
        q = qh * _sigmoid(qh) * (D_HEAD ** -0.5)
        l_hi, l_lo = _split_bf16(logf)
        b = _dot(tril_t, l_hi) + _dot(tril_t, l_lo)
        b3 = b.reshape(nch, CH, D_HEAD)
        b_mid = jnp.broadcast_to(b3[:, CH // 2 - 1:CH // 2, :], b3.shape).reshape(T, D_HEAD)
        b_last = jnp.broadcast_to(b3[:, CH - 1:CH, :], b3.shape).reshape(T, D_HEAD)
        bm = b - b_mid
        q_in = (q * jnp.exp(jnp.minimum(bm, EXP_CLAMP))).astype(BF16)
        k_in = (k * jnp.exp(jnp.minimum(-bm, EXP_CLAMP))).astype(BF16)
        a = jnp.where(causal_t, _dot_nt(q_in, k_in), 0.0)
        v = pr[:, _hs(R_IH, h)]
        bsc[:, _hs(0, h)] = _dot(a.astype(BF16), v.astype(BF16))
        pr[:, _hs(R_QH, h)] = q * jnp.exp(b)
        pr[:, _hs(R_FH, h)] = k * jnp.exp(b_last - b)
        pr[:, _hs(R_GA, h)] = jnp.exp(b_last)

    def hgrn_chunk(c, carry):
        r0 = pl.multiple_of(c * CH, CH)
        rows = pl.ds(r0, CH)
        st = [sht[h] for h in heads]
        o_in = [_dot_nt(pr[rows, _hs(R_QH, h)].astype(BF16), st[h].astype(BF16)) for h in heads]
        upd = [_dot_tn(pr[rows, _hs(R_IH, h)].astype(BF16), pr[rows, _hs(R_FH, h)].astype(BF16)) for h in heads]
        for h in heads:
            sht[h] = st[h] * pr[pl.ds(r0, 8), _hs(R_GA, h)][0:1, :] + upd[h]
            o = bsc[rows, _hs(0, h)] + o_in[h]
            ms = jnp.mean(o * o, axis=-1, keepdims=True)
            gate = pr[rows, _hs(R_GH, h)]
            ob_ref[rows, _hs(0, h)] = (o * lax.rsqrt(ms + RMS_EPS) * hnw * _sigmoid(gate)).astype(BF16)
        return carry

    lax.fori_loop(0, nch, hgrn_chunk, 0, unroll=min(4, nch))

    @pl.when(i == nt - 1)
    def _hgrn_state():
        for h in heads:
            sh_ref[h] = sht[h].T


def _ffn_kernel(FC, x_ref, oa_ref, ob_ref, mg_ref, wa_ref, wb_ref, wo_ref, g1_ref, b1_ref,
                wgu_ref, wd_ref, g2_ref, b2_ref, y_ref):
    ya = _dot(oa_ref[...], wa_ref[...])
    yb = _dot(ob_ref[...], wb_ref[...])
    merged = (_sigmoid(mg_ref[:, 0:D_MODEL].astype(F32)) * ya
              + _sigmoid(mg_ref[:, D_MODEL:2 * D_MODEL].astype(F32)) * yb)
    mix = _dot(merged.astype(BF16), wo_ref[...])
    x1 = _layer_norm(ALPHA * x_ref[...] + mix, g1_ref[...], b1_ref[...])
    x1b = x1.astype(BF16)
    acc = jnp.zeros(x1.shape, F32)
    for j in range(D_FF // FC):
        gate = _dot(x1b, wgu_ref[:, j * FC:(j + 1) * FC])
        up = _dot(x1b, wgu_ref[:, D_FF + j * FC:D_FF + (j + 1) * FC])
        act = (gate * _sigmoid(gate) * up).astype(BF16)
        acc = acc + _dot(act, wd_ref[j * FC:(j + 1) * FC, :])
    y_ref[...] = _layer_norm(ALPHA * x1 + acc, g2_ref[...], b2_ref[...])


def _const_spec(shape):
    nd = len(shape)
    return pl.BlockSpec(shape, lambda *_: (0,) * nd, pipeline_mode=pl.Buffered(1))


def _tiles(L):
    T = min(256, L)
    assert L % T == 0 and L >= CONV_W - 1
    CG = min(64, T)
    CH = min(16, T)
    assert T % CG == 0 and T % CH == 0 and CG & (CG - 1) == 0 and CH & (CH - 1) == 0 and CH >= 8
    return T, CG, CH


def _layer(layer, x, conv_buf, s_gdn, s_hgrn, w_in_p, conv_w, a_log, dt_bias, gdn_norm_w, lb_logits,
           hgrn_norm_w, w_br_a, w_br_b, w_out, ln1_g, ln1_b, w_gate_up, w_down, ln2_g, ln2_b):
    B, L, _ = x.shape
    T, CG, CH = _tiles(L)
    nt = L // T
    cache8 = jnp.pad(conv_buf.astype(F32), ((0, 0), (8 - (CONV_W - 1), 0), (0, 0)))
    row128 = lambda v: jnp.pad(v.astype(F32).reshape(1, -1), ((0, 0), (0, 128 - v.shape[-1])))

    seq = lambda w: pl.BlockSpec((None, T, w), lambda b, i: (b, i, 0))
    per_b = lambda *s: pl.BlockSpec((None,) + s, lambda b, i: (b,) + (0,) * len(s))
    oa, ob, mg, conv_new, sg_new, sh_new = pl.pallas_call(
        functools.partial(_mixer_kernel, layer, T, CG, CH, nt),
        grid=(B, nt),
        in_specs=[seq(D_MODEL), per_b(8, CONV_CH), per_b(N_HEADS, D_HEAD, D_HEAD), per_b(N_HEADS, D_HEAD, D_HEAD),
                  _const_spec((D_MODEL, P_TOTAL)), _const_spec((CONV_W, CONV_CH)), _const_spec((1, 128)),
                  _const_spec((1, 128)), _const_spec((1, 128)), _const_spec(lb_logits.shape),
                  _const_spec((1, 128))],
        out_specs=[seq(D_MODEL), seq(D_MODEL), seq(2 * D_MODEL), per_b(CONV_W - 1, CONV_CH),
                   per_b(N_HEADS, D_HEAD, D_HEAD), per_b(N_HEADS, D_HEAD, D_HEAD)],
        out_shape=[jax.ShapeDtypeStruct((B, L, D_MODEL), BF16), jax.ShapeDtypeStruct((B, L, D_MODEL), BF16),
                   jax.ShapeDtypeStruct((B, L, 2 * D_MODEL), BF16),
                   jax.ShapeDtypeStruct((B, CONV_W - 1, CONV_CH), F32),
                   jax.ShapeDtypeStruct((B, N_HEADS, D_HEAD, D_HEAD), F32),
                   jax.ShapeDtypeStruct((B, N_HEADS, D_HEAD, D_HEAD), F32)],
        scratch_shapes=[pltpu.VMEM((T + 8, CONV_CH), F32),
                        pltpu.VMEM((T, CONV_CH), F32),
                        pltpu.VMEM((T, R_TOTAL), F32),
                        pltpu.VMEM((T, D_MODEL), F32),
                        pltpu.VMEM((T, D_MODEL), F32),
                        pltpu.VMEM((T, D_MODEL), F32),
                        pltpu.VMEM((8 * (T // CG), 128), F32),
                        pltpu.VMEM((N_HEADS, D_HEAD, D_HEAD), F32)],
        compiler_params=pltpu.CompilerParams(dimension_semantics=("parallel", "arbitrary"),
                                             vmem_limit_bytes=VMEM_LIMIT_BYTES),
        name="mixer",
    )(x, cache8, s_gdn.astype(F32), s_hgrn.astype(F32), w_in_p, conv_w.astype(F32), row128(a_log),
      row128(dt_bias), gdn_norm_w.astype(F32).reshape(1, -1), lb_logits.astype(F32),
      hgrn_norm_w.astype(F32).reshape(1, -1))

    N = B * L
    T2 = min(256, N)
    assert N % T2 == 0
    FC = 1408
    tok = lambda w: pl.BlockSpec((T2, w), lambda i: (i, 0))
    row = lambda v: v.astype(F32).reshape(1, -1)
    y = pl.pallas_call(
        functools.partial(_ffn_kernel, FC),
        grid=(N // T2,),
        in_specs=[tok(D_MODEL), tok(D_MODEL), tok(D_MODEL), tok(2 * D_MODEL),
                  _const_spec((D_MODEL, D_MODEL)), _const_spec((D_MODEL, D_MODEL)), _const_spec((D_MODEL, D_MODEL)),
                  _const_spec((1, D_MODEL)), _const_spec((1, D_MODEL)),
                  _const_spec((D_MODEL, 2 * D_FF)), _const_spec((D_FF, D_MODEL)),
                  _const_spec((1, D_MODEL)), _const_spec((1, D_MODEL))],
        out_specs=tok(D_MODEL),
        out_shape=jax.ShapeDtypeStruct((N, D_MODEL), F32),
        compiler_params=pltpu.CompilerParams(dimension_semantics=("parallel",),
                                             vmem_limit_bytes=VMEM_LIMIT_BYTES),
        name="ffn",
    )(x.reshape(N, D_MODEL), oa.reshape(N, D_MODEL), ob.reshape(N, D_MODEL), mg.reshape(N, 2 * D_MODEL),
      w_br_a, w_br_b, w_out, row(ln1_g), row(ln1_b), w_gate_up, w_down, row(ln2_g), row(ln2_b))
    return y.reshape(B, L, D_MODEL), conv_new, sg_new, sh_new


def _permute_w_in(w):
    n_ab = 2 * N_HEADS
    head = w[:, 0:4 * D_MODEL]
    ab = w[:, 4 * D_MODEL:4 * D_MODEL + n_ab]
    tail = w[:, 4 * D_MODEL + n_ab:]
    pad = jnp.zeros((w.shape[0], 128 - n_ab), w.dtype)
    return jnp.concatenate([head, tail, ab, pad], axis=1).astype(BF16)


def kernel(x_prompt, x_sample, cache_gdn_conv, state_gdn, state_hgrn, w_in, conv_w, a_log, dt_bias, gdn_norm_w, hgrn_lb_logits, hgrn_norm_w, w_br_a, w_br_b, w_out, ln1_g, ln1_b, w_gate_up, w_down, ln2_g, ln2_b):
    depth = w_in.shape[0]
    assert depth == 1, "ALPHA is baked for a single layer"
    B = x_prompt.shape[0]
    dt = x_prompt.dtype
    y_p, y_s = x_prompt, x_sample
    outs = [[] for _ in range(6)]
    for l in range(depth):
        wl = (_permute_w_in(w_in[l]), conv_w[l], a_log[l], dt_bias[l], gdn_norm_w[l], hgrn_lb_logits,
              hgrn_norm_w[l], w_br_a[l].astype(BF16), w_br_b[l].astype(BF16), w_out[l].astype(BF16),
              ln1_g[l], ln1_b[l], w_gate_up[l].astype(BF16), w_down[l].astype(BF16), ln2_g[l], ln2_b[l])
        zc = jnp.zeros((B, CONV_W - 1, CONV_CH), F32)
        zs = jnp.zeros((B, N_HEADS, D_HEAD, D_HEAD), F32)
        y_p, cp, gp, hp = _layer(l, y_p, zc, zs, zs, *wl)
        y_s, cs, gs, hs = _layer(l, y_s, cache_gdn_conv[l], state_gdn[l], state_hgrn[l], *wl)
        for lst, v in zip(outs, (cp, gp, hp, cs, gs, hs)):
            lst.append(v.astype(dt))
    return (y_p.astype(dt), y_s.astype(dt)) + tuple(jnp.stack(o) for o in outs)
```

```python
import functools
import math

import jax
import jax.numpy as jnp
from jax import lax
from jax.experimental import pallas as pl
from jax.experimental.pallas import tpu as pltpu

F32 = jnp.float32
BF16 = jnp.bfloat16

D_MODEL = 1024
N_HEADS = 8
D_HEAD = 128
CONV_W = 4
CONV_CH = 3 * D_MODEL
D_FF = 2816
ALPHA = 2.0 ** 0.25
LN_EPS = 1e-5
RMS_EPS = 1e-6
L2_EPS = 1e-6
EXP_CLAMP = 80.0
GDN_BATCH = 4

P_QKV = 0
P_REST = CONV_CH
P_MG = P_REST + 5 * D_MODEL
P_AB = P_MG + 2 * D_MODEL
P_TOTAL = P_AB + 128
R_GA, R_QH, R_FH, R_IH, R_GH, R_AB = 0, 1024, 2048, 3072, 4096, 5120
R_TOTAL = R_AB + 128
N_SLABS = CONV_CH // 128
Q_SLAB, K_SLAB, V_SLAB = 0, N_HEADS, 2 * N_HEADS

VMEM_LIMIT_BYTES = 60 * 1024 * 1024


def _dot(a, b):
    return jnp.dot(a, b, preferred_element_type=F32)


def _dot_nt(a, b):
    return lax.dot_general(a, b, (((1,), (1,)), ((), ())), preferred_element_type=F32)


def _dot_tn(a, b):
    return lax.dot_general(a, b, (((0,), (0,)), ((), ())), preferred_element_type=F32)


def _sigmoid(x):
    return 1.0 / (1.0 + jnp.exp(-x))


def _split_bf16(x):
    hi = x.astype(BF16)
    lo = (x - hi.astype(F32)).astype(BF16)
    return hi, lo


def _layer_norm(x, g, b):
    mu = jnp.mean(x, axis=-1, keepdims=True)
    xc = x - mu
    var = jnp.mean(xc * xc, axis=-1, keepdims=True)
    return xc * lax.rsqrt(var + LN_EPS) * g + b


def _hs(off, h):
    return slice(off + h * D_HEAD, off + (h + 1) * D_HEAD)


def _mixer_kernel(layer, T, CG, CH, nt,
                  x_ref, cache_ref, sg0_ref, sh0_ref, w_ref, cw_ref, alog_ref, dtb_ref, gnw_ref,
                  lbl_ref, hnw_ref,
                  oa_ref, ob_ref, mg_ref, conv_ref, sg_ref, sh_ref,
                  cb, qkv, pr, kcs, aqs, els, sht):
    i = pl.program_id(1)
    heads = range(N_HEADS)

    @pl.when(i == 0)
    def _init():
        for blk in range(N_SLABS):
            cb[blk, 0:8, :] = cache_ref[:, blk * 128:(blk + 1) * 128]
        sg_ref[...] = sg0_ref[...]
        for h in heads:
            sht[h] = sh0_ref[h].T

    xb = x_ref[...].astype(BF16)
    for s in range(3):
        u = _dot(xb, w_ref[:, P_QKV + s * 1024:P_QKV + (s + 1) * 1024])
        for j in range(N_HEADS):
            cb[s * N_HEADS + j, 8:8 + T, :] = u[:, j * 128:(j + 1) * 128]
    for s in range(5):
        pr[:, s * 1024:(s + 1) * 1024] = _dot(xb, w_ref[:, P_REST + s * 1024:P_REST + (s + 1) * 1024])
    for s in range(2):
        mg_ref[:, s * 1024:(s + 1) * 1024] = _dot(xb, w_ref[:, P_MG + s * 1024:P_MG + (s + 1) * 1024]).astype(BF16)
    pr[:, R_AB:R_AB + 128] = _dot(xb, w_ref[:, P_AB:P_AB + 128])

    for blk in range(N_SLABS):
        cs = slice(blk * 128, (blk + 1) * 128)
        taps = {m: cb[blk, pl.ds(m, T // 8, stride=8), :] for m in range(5, 16)}
        wj = [cw_ref[j:j + 1, cs] for j in range(CONV_W)]
        for r in range(8):
            y = taps[r + 5] * wj[0] + taps[r + 6] * wj[1] + taps[r + 7] * wj[2] + taps[r + 8] * wj[3]
            y = y * _sigmoid(y)
            if blk < 2 * N_HEADS:
                y = y * lax.rsqrt(jnp.sum(y * y, axis=-1, keepdims=True) + L2_EPS)
                if blk < N_HEADS:
                    y = y * (D_HEAD ** -0.5)
            qkv[blk, pl.ds(r, T // 8, stride=8), :] = y

    @pl.when(i == nt - 1)
    def _conv_state():
        for blk in range(N_SLABS):
            conv_ref[:, blk * 128:(blk + 1) * 128] = cb[blk, T + 5:T + 8, :]

    cb[:, 0:8, :] = cb[:, T:T + 8, :]

    neg_a = -jnp.exp(alog_ref[...])
    dtb = dtb_ref[...]
    gnw = gnw_ref[...]
    lane_g = lax.broadcasted_iota(jnp.int32, (CG, 128), 1)
    row_g = lax.broadcasted_iota(jnp.int32, (CG, CG), 0)
    col_g = lax.broadcasted_iota(jnp.int32, (CG, CG), 1)
    incl_g = row_g >= col_g
    strict_g = row_g > col_g
    tril_g = incl_g.astype(BF16)
    n_neumann = int(math.log2(CG)) - 1

    NB = min(GDN_BATCH, T // CG)
    probs = [(j, h) for j in range(NB) for h in heads]

    def gdn_prepare(it, carry):
        rows, gc, gc_t, egc, e_to_end, beta = [], [], [], [], [], []
        for j in range(NB):
            c = it * NB + j
            rows.append(pl.ds(pl.multiple_of(c * CG, CG), CG))
            ab = pr[rows[j], R_AB:R_AB + 128]
            xa = ab + dtb
            softplus = jnp.maximum(xa, 0.0) + jnp.log(1.0 + jnp.exp(-jnp.abs(xa)))
            g = jnp.where(lane_g < N_HEADS, neg_a * softplus, 0.0)
            beta.append(_sigmoid(ab))
            g_hi, g_lo = _split_bf16(g)
            gcj = _dot(tril_g, g_hi) + _dot(tril_g, g_lo)
            gc.append(gcj)
            gc_sq = jnp.concatenate([gcj, jnp.zeros((128 - CG, 128), F32)], axis=0) if CG < 128 else gcj
            gc_t.append(gc_sq.T)
            egc.append(jnp.exp(gcj))
            g_last = gcj[CG - 1:CG, :]
            e_to_end.append(jnp.exp(g_last - gcj))
            els[pl.ds(pl.multiple_of(c * 8, 8), 8), :] = jnp.broadcast_to(jnp.exp(g_last), (8, 128))

        q = [qkv[Q_SLAB + h, rows[j], :] for j, h in probs]
        k = [qkv[K_SLAB + h, rows[j], :] for j, h in probs]
        v = [qkv[V_SLAB + h, rows[j], :] for j, h in probs]
        b_col = [beta[j][:, N_HEADS + h:N_HEADS + h + 1] for j, h in probs]
        eg_col = [egc[j][:, h:h + 1] for j, h in probs]
        dec = [jnp.exp(jnp.minimum(gc[j][:, h:h + 1] - gc_t[j][h:h + 1, 0:CG], 0.0)) for j, h in probs]
        n = range(len(probs))
        kb = [k[i] * b_col[i] for i in n]
        m = [_dot_nt(jnp.concatenate([kb[i], q[i]], axis=0).astype(BF16), k[i].astype(BF16)) for i in n]
        lp = [jnp.where(strict_g, m[i][0:CG] * dec[i], 0.0) for i in n]
        for i, (j, h) in enumerate(probs):
            aqs[rows[j], h * D_HEAD:h * D_HEAD + CG] = jnp.where(incl_g, m[i][CG:2 * CG] * dec[i], 0.0)
        p = [-lp[i] for i in n]
        lpb = [lp[i].astype(BF16) for i in n]
        for step in range(n_neumann):
            lp = [_dot(lpb[i], lpb[i]) for i in n]
            lpb = [lp[i].astype(BF16) for i in n]
            p = [p[i] + lp[i] + _dot(p[i].astype(BF16), lpb[i]) for i in n]
        rhs = [jnp.concatenate([v[i] * b_col[i], kb[i] * eg_col[i]], axis=1) for i in n]
        sol = [rhs[i] + _dot(p[i].astype(BF16), rhs[i].astype(BF16)) for i in n]
        for i, (j, h) in enumerate(probs):
            qkv[V_SLAB + h, rows[j], :] = sol[i][:, 0:D_HEAD]
            kcs[rows[j], _hs(0, h)] = sol[i][:, D_HEAD:2 * D_HEAD]
            qkv[Q_SLAB + h, rows[j], :] = q[i] * eg_col[i]
            qkv[K_SLAB + h, rows[j], :] = k[i] * e_to_end[j][:, h:h + 1]
        return carry

    lax.fori_loop(0, (T // CG) // NB, gdn_prepare, 0)

    def gdn_recur(c, carry):
        rows = slice(c * CG, (c + 1) * CG)
        e_last = els[c * 8:c * 8 + 1, :]
        s = [sg_ref[h] for h in heads]
        lhs = [jnp.concatenate([kcs[rows, _hs(0, h)], qkv[Q_SLAB + h, rows, :]], axis=0).astype(BF16) for h in heads]
        qs = [_dot(lhs[h], s[h].astype(BF16)) for h in heads]
        v_new = [(qkv[V_SLAB + h, rows, :] - qs[h][0:CG]).astype(BF16) for h in heads]
        o = [qs[h][CG:2 * CG] + _dot(aqs[rows, h * D_HEAD:h * D_HEAD + CG].astype(BF16), v_new[h]) for h in heads]
        upd = [_dot_tn(qkv[K_SLAB + h, rows, :].astype(BF16), v_new[h]) for h in heads]
        for h in heads:
            sg_ref[h] = s[h] * e_last[:, h:h + 1] + upd[h]
            ms = jnp.mean(o[h] * o[h], axis=-1, keepdims=True)
            gate = pr[rows, _hs(R_GA, h)]
            oa_ref[rows, _hs(0, h)] = (o[h] * lax.rsqrt(ms + RMS_EPS) * gnw * (gate * _sigmoid(gate))).astype(BF16)
        return carry

    for c in range(T // CG):
        gdn_recur(c, 0)

    ll = lbl_ref[...]
    el = jnp.exp(ll - jnp.max(ll, axis=0, keepdims=True))
    lb = jnp.sum(el[0:layer + 1], axis=0, keepdims=True) / jnp.sum(el, axis=0, keepdims=True)
    hnw = hnw_ref[...]
    row_t = lax.broadcasted_iota(jnp.int32, (T, T), 0)
    col_t = lax.broadcasted_iota(jnp.int32, (T, T), 1)
    xor_t = row_t ^ col_t
    before_t = col_t < row_t
    causal_t = row_t >= col_t
    near_t = causal_t & (xor_t < CH)
    tril_t = causal_t.astype(BF16)
    row_i = lax.broadcasted_iota(jnp.int32, (T, D_HEAD), 0)
    widths = [w for w in (2 ** e for e in range(1, 16)) if CH < w <= T]
    level_mask = [before_t & (xor_t >= w // 2) & (xor_t < w) for w in widths]
    upper_half = [(row_i & (w - 1)) >= w // 2 for w in widths]

    def block_row(x, w, r):
        x3 = x.reshape(T // w, w, D_HEAD)
        return jnp.broadcast_to(x3[:, r:r + 1, :], x3.shape).reshape(T, D_HEAD)

    for h in heads:
        lbh = lb[:, _hs(0, h)]
        z = pr[:, _hs(R_FH, h)]
        e = jnp.exp(-jnp.abs(z))
        r = 1.0 / (1.0 + e)
        sig_pos = jnp.where(z >= 0.0, r, e * r)
        sig_neg = jnp.where(z >= 0.0, e * r, r)
        logf = jnp.log(lbh + (1.0 - lbh) * sig_pos)
        k = (1.0 - lbh) * sig_neg
        qh = pr[:, _hs(R_QH, h)]
        q = qh * _sigmoid(qh) * (D_HEAD ** -0.5)
        vb = pr[:, _hs(R_IH, h)].astype(BF16)
        l_hi, l_lo = _split_bf16(logf)
        b = _dot(tril_t, l_hi) + _dot(tril_t, l_lo)
        bm = b - block_row(b, CH, CH // 2 - 1)
        q_in = (q * jnp.exp(jnp.minimum(bm, EXP_CLAMP))).astype(BF16)
        k_in = (k * jnp.exp(jnp.minimum(-bm, EXP_CLAMP))).astype(BF16)
        a = jnp.where(near_t, _dot_nt(q_in, k_in), 0.0)
        for w, mask, upper in zip(widths, level_mask, upper_half):
            d = b - block_row(b, w, w // 2 - 1)
            ed = jnp.exp(-jnp.abs(d))
            q_w = jnp.where(upper, q * ed, 0.0).astype(BF16)
            k_w = jnp.where(upper, 0.0, k * ed).astype(BF16)
            a = jnp.where(mask, _dot_nt(q_w, k_w), a)
        b_end = b[T - 1:T, :]
        st = sht[h]
        o = _dot(a.astype(BF16), vb) + _dot_nt((q * jnp.exp(b)).astype(BF16), st.astype(BF16))
        sht[h] = st * jnp.exp(b_end) + _dot_tn(vb, (k * jnp.exp(b_end - b)).astype(BF16))
        ms = jnp.mean(o * o, axis=-1, keepdims=True)
        gate = pr[:, _hs(R_GH, h)]
        ob_ref[:, _hs(0, h)] = (o * lax.rsqrt(ms + RMS_EPS) * hnw * _sigmoid(gate)).astype(BF16)

    @pl.when(i == nt - 1)
    def _hgrn_state():
        for h in heads:
            sh_ref[h] = sht[h].T


def _ffn_kernel(FC, x_ref, oa_ref, ob_ref, mg_ref, wa_ref, wb_ref, wo_ref, g1_ref, b1_ref,
                wgu_ref, wd_ref, g2_ref, b2_ref, y_ref):
    ya = _dot(oa_ref[...], wa_ref[...])
    yb = _dot(ob_ref[...], wb_ref[...])
    merged = (_sigmoid(mg_ref[:, 0:D_MODEL].astype(F32)) * ya
              + _sigmoid(mg_ref[:, D_MODEL:2 * D_MODEL].astype(F32)) * yb)
    mix = _dot(merged.astype(BF16), wo_ref[...])
    x1 = _layer_norm(ALPHA * x_ref[...] + mix, g1_ref[...], b1_ref[...])
    x1b = x1.astype(BF16)
    acc = jnp.zeros(x1.shape, F32)
    for j in range(D_FF // FC):
        gate = _dot(x1b, wgu_ref[:, j * FC:(j + 1) * FC])
        up = _dot(x1b, wgu_ref[:, D_FF + j * FC:D_FF + (j + 1) * FC])
        act = (gate * _sigmoid(gate) * up).astype(BF16)
        acc = acc + _dot(act, wd_ref[j * FC:(j + 1) * FC, :])
    y_ref[...] = _layer_norm(ALPHA * x1 + acc, g2_ref[...], b2_ref[...])


def _const_spec(shape):
    nd = len(shape)
    return pl.BlockSpec(shape, lambda *_: (0,) * nd, pipeline_mode=pl.Buffered(1))


def _tiles(L):
    T = min(256, L)
    assert L % T == 0 and L >= CONV_W - 1
    CG = min(64, T)
    CH = min(16, T)
    assert T % CG == 0 and T % CH == 0 and CG & (CG - 1) == 0 and CH & (CH - 1) == 0 and CH >= 8
    return T, CG, CH


def _layer(layer, x, conv_buf, s_gdn, s_hgrn, w_in_p, conv_w, a_log, dt_bias, gdn_norm_w, lb_logits,
           hgrn_norm_w, w_br_a, w_br_b, w_out, ln1_g, ln1_b, w_gate_up, w_down, ln2_g, ln2_b):
    B, L, _ = x.shape
    T, CG, CH = _tiles(L)
    nt = L // T
    cache8 = jnp.pad(conv_buf.astype(F32), ((0, 0), (8 - (CONV_W - 1), 0), (0, 0)))
    row128 = lambda v: jnp.pad(v.astype(F32).reshape(1, -1), ((0, 0), (0, 128 - v.shape[-1])))

    seq = lambda w: pl.BlockSpec((None, T, w), lambda b, i: (b, i, 0))
    per_b = lambda *s: pl.BlockSpec((None,) + s, lambda b, i: (b,) + (0,) * len(s))
    oa, ob, mg, conv_new, sg_new, sh_new = pl.pallas_call(
        functools.partial(_mixer_kernel, layer, T, CG, CH, nt),
        grid=(B, nt),
        in_specs=[seq(D_MODEL), per_b(8, CONV_CH), per_b(N_HEADS, D_HEAD, D_HEAD), per_b(N_HEADS, D_HEAD, D_HEAD),
                  _const_spec((D_MODEL, P_TOTAL)), _const_spec((CONV_W, CONV_CH)), _const_spec((1, 128)),
                  _const_spec((1, 128)), _const_spec((1, 128)), _const_spec(lb_logits.shape),
                  _const_spec((1, 128))],
        out_specs=[seq(D_MODEL), seq(D_MODEL), seq(2 * D_MODEL), per_b(CONV_W - 1, CONV_CH),
                   per_b(N_HEADS, D_HEAD, D_HEAD), per_b(N_HEADS, D_HEAD, D_HEAD)],
        out_shape=[jax.ShapeDtypeStruct((B, L, D_MODEL), BF16), jax.ShapeDtypeStruct((B, L, D_MODEL), BF16),
                   jax.ShapeDtypeStruct((B, L, 2 * D_MODEL), BF16),
                   jax.ShapeDtypeStruct((B, CONV_W - 1, CONV_CH), F32),
                   jax.ShapeDtypeStruct((B, N_HEADS, D_HEAD, D_HEAD), F32),
                   jax.ShapeDtypeStruct((B, N_HEADS, D_HEAD, D_HEAD), F32)],
        scratch_shapes=[pltpu.VMEM((N_SLABS, T + 8, 128), F32),
                        pltpu.VMEM((N_SLABS, T, 128), F32),
                        pltpu.VMEM((T, R_TOTAL), F32),
                        pltpu.VMEM((T, D_MODEL), F32),
                        pltpu.VMEM((T, D_MODEL), F32),
                        pltpu.VMEM((8 * (T // CG), 128), F32),
                        pltpu.VMEM((N_HEADS, D_HEAD, D_HEAD), F32)],
        compiler_params=pltpu.CompilerParams(dimension_semantics=("parallel", "arbitrary"),
                                             vmem_limit_bytes=VMEM_LIMIT_BYTES),
        name="mixer",
    )(x, cache8, s_gdn.astype(F32), s_hgrn.astype(F32), w_in_p, conv_w.astype(F32), row128(a_log),
      row128(dt_bias), gdn_norm_w.astype(F32).reshape(1, -1), lb_logits.astype(F32),
      hgrn_norm_w.astype(F32).reshape(1, -1))

    N = B * L
    T2 = min(256, N)
    assert N % T2 == 0
    FC = 1408
    tok = lambda w: pl.BlockSpec((T2, w), lambda i: (i, 0))
    row = lambda v: v.astype(F32).reshape(1, -1)
    y = pl.pallas_call(
        functools.partial(_ffn_kernel, FC),
        grid=(N // T2,),
        in_specs=[tok(D_MODEL), tok(D_MODEL), tok(D_MODEL), tok(2 * D_MODEL),
                  _const_spec((D_MODEL, D_MODEL)), _const_spec((D_MODEL, D_MODEL)), _const_spec((D_MODEL, D_MODEL)),
                  _const_spec((1, D_MODEL)), _const_spec((1, D_MODEL)),
                  _const_spec((D_MODEL, 2 * D_FF)), _const_spec((D_FF, D_MODEL)),
                  _const_spec((1, D_MODEL)), _const_spec((1, D_MODEL))],
        out_specs=tok(D_MODEL),
        out_shape=jax.ShapeDtypeStruct((N, D_MODEL), F32),
        compiler_params=pltpu.CompilerParams(dimension_semantics=("parallel",),
                                             vmem_limit_bytes=VMEM_LIMIT_BYTES),
        name="ffn",
    )(x.reshape(N, D_MODEL), oa.reshape(N, D_MODEL), ob.reshape(N, D_MODEL), mg.reshape(N, 2 * D_MODEL),
      w_br_a, w_br_b, w_out, row(ln1_g), row(ln1_b), w_gate_up, w_down, row(ln2_g), row(ln2_b))
    return y.reshape(B, L, D_MODEL), conv_new, sg_new, sh_new


def _permute_w_in(w):
    n_ab = 2 * N_HEADS
    head = w[:, 0:4 * D_MODEL]
    ab = w[:, 4 * D_MODEL:4 * D_MODEL + n_ab]
    tail = w[:, 4 * D_MODEL + n_ab:]
    pad = jnp.zeros((w.shape[0], 128 - n_ab), w.dtype)
    return jnp.concatenate([head, tail, ab, pad], axis=1).astype(BF16)


def kernel(x_prompt, x_sample, cache_gdn_conv, state_gdn, state_hgrn, w_in, conv_w, a_log, dt_bias, gdn_norm_w, hgrn_lb_logits, hgrn_norm_w, w_br_a, w_br_b, w_out, ln1_g, ln1_b, w_gate_up, w_down, ln2_g, ln2_b):
    depth = w_in.shape[0]
    assert depth == 1, "ALPHA is baked for a single layer"
    B = x_prompt.shape[0]
    dt = x_prompt.dtype
    y_p, y_s = x_prompt, x_sample
    outs = [[] for _ in range(6)]
    for l in range(depth):
        wl = (_permute_w_in(w_in[l]), conv_w[l], a_log[l], dt_bias[l], gdn_norm_w[l], hgrn_lb_logits,
              hgrn_norm_w[l], w_br_a[l].astype(BF16), w_br_b[l].astype(BF16), w_out[l].astype(BF16),
              ln1_g[l], ln1_b[l], w_gate_up[l].astype(BF16), w_down[l].astype(BF16), ln2_g[l], ln2_b[l])
        zc = jnp.zeros((B, CONV_W - 1, CONV_CH), F32)
        zs = jnp.zeros((B, N_HEADS, D_HEAD, D_HEAD), F32)
        y_p, cp, gp, hp = _layer(l, y_p, zc, zs, zs, *wl)
        y_s, cs, gs, hs = _layer(l, y_s, cache_gdn_conv[l], state_gdn[l], state_hgrn[l], *wl)
        for lst, v in zip(outs, (cp, gp, hp, cs, gs, hs)):
            lst.append(v.astype(dt))
    return (y_p.astype(dt), y_s.astype(dt)) + tuple(jnp.stack(o) for o in outs)
```

```python
import functools
import math

import jax
import jax.numpy as jnp
from jax import lax
from jax.experimental import pallas as pl
from jax.experimental.pallas import tpu as pltpu

F32 = jnp.float32
BF16 = jnp.bfloat16

D_MODEL = 1024
N_HEADS = 8
D_HEAD = 128
CONV_W = 4
CONV_CH = 3 * D_MODEL
D_FF = 2816
ALPHA = 2.0 ** 0.25
LN_EPS = 1e-5
RMS_EPS = 1e-6
L2_EPS = 1e-6
EXP_CLAMP = 80.0
GDN_BATCH = 4

P_QKV = 0
P_REST = CONV_CH
P_MG = P_REST + 5 * D_MODEL
P_AB = P_MG + 2 * D_MODEL
P_TOTAL = P_AB + 128
P_QH, P_FH, P_IH, P_GH = (P_REST + s * D_MODEL for s in range(1, 5))
R_GA, R_AB = 0, 1024
R_TOTAL = R_AB + 128
R_QH, R_FH, R_IH, R_GH = 0, 1024, 2048, 3072
PIECE = 256
N_SLABS = CONV_CH // 128
Q_SLAB, K_SLAB, V_SLAB = 0, N_HEADS, 2 * N_HEADS

VMEM_LIMIT_BYTES = 60 * 1024 * 1024


def _dot(a, b):
    return jnp.dot(a, b, preferred_element_type=F32)


def _dot_nt(a, b):
    return lax.dot_general(a, b, (((1,), (1,)), ((), ())), preferred_element_type=F32)


def _dot_tn(a, b):
    return lax.dot_general(a, b, (((0,), (0,)), ((), ())), preferred_element_type=F32)


def _sigmoid(x):
    return 1.0 / (1.0 + jnp.exp(-x))


def _split_bf16(x):
    hi = x.astype(BF16)
    lo = (x - hi.astype(F32)).astype(BF16)
    return hi, lo


def _layer_norm(x, g, b):
    mu = jnp.mean(x, axis=-1, keepdims=True)
    xc = x - mu
    var = jnp.mean(xc * xc, axis=-1, keepdims=True)
    return xc * lax.rsqrt(var + LN_EPS) * g + b


def _hs(off, h):
    return slice(off + h * D_HEAD, off + (h + 1) * D_HEAD)


def _mixer_kernel(layer, BB, T, CG, CH, nt,
                  x_ref, cache_ref, sg0_ref, sh0_ref, w_ref, cw_ref, alog_ref, dtb_ref, gnw_ref,
                  lbl_ref, hnw_ref,
                  oa_ref, ob_ref, mg_ref, conv_ref, sg_ref, sh_ref,
                  xbs, cb, qkv, pr, hp, kcs, aqs, els, sht):
    i = pl.program_id(1)
    heads = range(N_HEADS)
    seqs = range(BB)
    R = BB * T

    @pl.when(i == 0)
    def _init():
        for bb in seqs:
            for blk in range(N_SLABS):
                cb[bb, blk, 0:8, :] = cache_ref[bb, :, blk * 128:(blk + 1) * 128]
            for h in heads:
                sht[bb, h] = sh0_ref[bb, h].T
        sg_ref[...] = sg0_ref[...]

    xbs[...] = x_ref[...].reshape(R, D_MODEL).astype(BF16)

    def project(c0, width):
        return _dot(xbs[...], w_ref[:, c0:c0 + width])

    pending = []

    def add_pieces(tag, c0, width, store):
        for c in range(0, width, PIECE):
            pending.append((tag, lambda c=c: store(c, project(c0 + c, PIECE))))

    def fill(n=1):
        for _ in range(min(n, len(pending))):
            pending.pop(0)[1]()

    def need(tag):
        while any(t == tag for t, _ in pending):
            pending.pop(0)[1]()

    def store_hp(off):
        def store(c, val):
            hp[:, off + c:off + c + PIECE] = val
        return store

    def store_ga(c, val):
        pr[:, R_GA + c:R_GA + c + PIECE] = val

    def store_mg(c, val):
        v16 = val.astype(BF16)
        for bb in seqs:
            mg_ref[bb, :, c:c + PIECE] = v16[bb * T:(bb + 1) * T]

    def add_pair(p):
        for s, c0 in enumerate((P_QH, P_FH, P_IH, P_GH)):
            add_pieces(("pair", p), c0 + p * 2 * D_HEAD, 2 * D_HEAD, store_hp(s * D_MODEL + p * 2 * D_HEAD))

    add_pair(0)
    add_pair(1)
    add_pieces("gate_a", P_REST, D_MODEL, store_ga)
    add_pair(2)
    add_pair(3)
    add_pieces("merge", P_MG, 2 * D_MODEL, store_mg)

    pr[:, R_AB:R_AB + 128] = project(P_AB, 128)
    for s in range(3):
        u = project(P_QKV + s * 1024, 1024)
        for bb in seqs:
            for j in heads:
                cb[bb, s * N_HEADS + j, 8:8 + T, :] = u[bb * T:(bb + 1) * T, j * 128:(j + 1) * 128]

    for blk in range(N_SLABS):
        cs = slice(blk * 128, (blk + 1) * 128)
        wj = [cw_ref[j:j + 1, cs] for j in range(CONV_W)]
        for bb in seqs:
            taps = {m: cb[bb, blk, pl.ds(m, T // 8, stride=8), :] for m in range(5, 16)}
            for r in range(8):
                y = taps[r + 5] * wj[0] + taps[r + 6] * wj[1] + taps[r + 7] * wj[2] + taps[r + 8] * wj[3]
                y = y * _sigmoid(y)
                if blk < 2 * N_HEADS:
                    y = y * lax.rsqrt(jnp.sum(y * y, axis=-1, keepdims=True) + L2_EPS)
                    if blk < N_HEADS:
                        y = y * (D_HEAD ** -0.5)
                qkv[blk, pl.ds(bb * T + r, T // 8, stride=8), :] = y
        if blk >= N_HEADS and blk % 2 == 1:
            fill()

    @pl.when(i == nt - 1)
    def _conv_state():
        for bb in seqs:
            for blk in range(N_SLABS):
                conv_ref[bb, :, blk * 128:(blk + 1) * 128] = cb[bb, blk, T + 5:T + 8, :]

    cb[:, :, 0:8, :] = cb[:, :, T:T + 8, :]

    ll = lbl_ref[...]
    el = jnp.exp(ll - jnp.max(ll, axis=0, keepdims=True))
    lb = jnp.sum(el[0:layer + 1], axis=0, keepdims=True) / jnp.sum(el, axis=0, keepdims=True)
    hnw = hnw_ref[...]
    row_t = lax.broadcasted_iota(jnp.int32, (T, T), 0)
    col_t = lax.broadcasted_iota(jnp.int32, (T, T), 1)
    xor_t = row_t ^ col_t
    before_t = col_t < row_t
    causal_t = row_t >= col_t
    near_t = causal_t & (xor_t < CH)
    tril_t = causal_t.astype(BF16)
    widths = [w for w in (2 ** e for e in range(1, 16)) if CH < w <= T]
    level_mask = [before_t & (xor_t >= w // 2) & (xor_t < w) for w in widths]

    def block_row(x, w, r):
        x3 = x.reshape(T // w, w, D_HEAD)
        return jnp.broadcast_to(x3[:, r:r + 1, :], x3.shape).reshape(T, D_HEAD)

    def hgrn_head(h):
        need(("pair", h // 2))
        lbh = lb[:, _hs(0, h)]
        rows = [slice(bb * T, (bb + 1) * T) for bb in seqs]
        z = [hp[rows[bb], _hs(R_FH, h)] for bb in seqs]
        e = [jnp.exp(-jnp.abs(z[bb])) for bb in seqs]
        r = [1.0 / (1.0 + e[bb]) for bb in seqs]
        sig_pos = [jnp.where(z[bb] >= 0.0, r[bb], e[bb] * r[bb]) for bb in seqs]
        sig_neg = [jnp.where(z[bb] >= 0.0, e[bb] * r[bb], r[bb]) for bb in seqs]
        logf = [jnp.log(lbh + (1.0 - lbh) * sig_pos[bb]) for bb in seqs]
        k = [(1.0 - lbh) * sig_neg[bb] for bb in seqs]
        qh = [hp[rows[bb], _hs(R_QH, h)] for bb in seqs]
        q = [qh[bb] * _sigmoid(qh[bb]) * (D_HEAD ** -0.5) for bb in seqs]
        vb = [hp[rows[bb], _hs(R_IH, h)].astype(BF16) for bb in seqs]
        split = [_split_bf16(logf[bb]) for bb in seqs]
        b = [_dot(tril_t, split[bb][0]) + _dot(tril_t, split[bb][1]) for bb in seqs]
        fill()
        bm =[b[bb] - block_row(b[bb], CH, CH // 2 - 1) for bb in seqs]
        q_in = [(q[bb] * jnp.exp(jnp.minimum(bm[bb], EXP_CLAMP))).astype(BF16) for bb in seqs]
        k_in = [(k[bb] * jnp.exp(jnp.minimum(-bm[bb], EXP_CLAMP))).astype(BF16) for bb in seqs]
        a = [jnp.where(near_t, _dot_nt(q_in[bb], k_in[bb]), 0.0) for bb in seqs]
        for w, mask in zip(widths, level_mask):
            hw = w // 2
            q_w, k_w = [], []
            for bb in seqs:
                b3, q3, k3 = (x.reshape(T // w, w, D_HEAD) for x in (b[bb], q[bb], k[bb]))
                ref = b3[:, hw - 1:hw, :]
                q_up = (q3[:, hw:, :] * jnp.exp(b3[:, hw:, :] - ref)).astype(BF16)
                k_lo = (k3[:, :hw, :] * jnp.exp(ref - b3[:, :hw, :])).astype(BF16)
                zero = jnp.zeros((T // w, hw, D_HEAD), BF16)
                q_w.append(jnp.concatenate([zero, q_up], axis=1).reshape(T, D_HEAD))
                k_w.append(jnp.concatenate([k_lo, zero], axis=1).reshape(T, D_HEAD))
            a = [jnp.where(mask, _dot_nt(q_w[bb], k_w[bb]), a[bb]) for bb in seqs]
        fill()
        b_end =[b[bb][T - 1:T, :] for bb in seqs]
        st = [sht[bb, h] for bb in seqs]
        q_st = [(q[bb] * jnp.exp(b[bb])).astype(BF16) for bb in seqs]
        k_st = [(k[bb] * jnp.exp(b_end[bb] - b[bb])).astype(BF16) for bb in seqs]
        o = [_dot(a[bb].astype(BF16), vb[bb]) + _dot_nt(q_st[bb], st[bb].astype(BF16)) for bb in seqs]
        upd = [_dot_tn(vb[bb], k_st[bb]) for bb in seqs]
        for bb in seqs:
            sht[bb, h] = st[bb] * jnp.exp(b_end[bb]) + upd[bb]
            ms = jnp.mean(o[bb] * o[bb], axis=-1, keepdims=True)
            gate = hp[rows[bb], _hs(R_GH, h)]
            ob_ref[bb, :, _hs(0, h)] = (o[bb] * lax.rsqrt(ms + RMS_EPS) * hnw * _sigmoid(gate)).astype(BF16)

    neg_a = -jnp.exp(alog_ref[...])
    dtb = dtb_ref[...]
    gnw = gnw_ref[...]
    lane_g = lax.broadcasted_iota(jnp.int32, (CG, 128), 1)
    row_g = lax.broadcasted_iota(jnp.int32, (CG, CG), 0)
    col_g = lax.broadcasted_iota(jnp.int32, (CG, CG), 1)
    incl_g = row_g >= col_g
    strict_g = row_g > col_g
    tril_g = incl_g.astype(BF16)
    n_neumann = int(math.log2(CG)) - 1

    NB = min(GDN_BATCH, R // CG)
    probs = [(j, h) for j in range(NB) for h in heads]

    def gdn_prepare(it):
        rows, gc, gc_t, egc, e_to_end, beta = [], [], [], [], [], []
        for j in range(NB):
            c = it * NB + j
            rows.append(slice(c * CG, (c + 1) * CG))
            ab = pr[rows[j], R_AB:R_AB + 128]
            xa = ab + dtb
            softplus = jnp.maximum(xa, 0.0) + jnp.log(1.0 + jnp.exp(-jnp.abs(xa)))
            g = jnp.where(lane_g < N_HEADS, neg_a * softplus, 0.0)
            beta.append(_sigmoid(ab))
            g_hi, g_lo = _split_bf16(g)
            gcj = _dot(tril_g, g_hi) + _dot(tril_g, g_lo)
            gc.append(gcj)
            gc_sq = jnp.concatenate([gcj, jnp.zeros((128 - CG, 128), F32)], axis=0) if CG < 128 else gcj
            gc_t.append(gc_sq.T)
            egc.append(jnp.exp(gcj))
            g_last = gcj[CG - 1:CG, :]
            e_to_end.append(jnp.exp(g_last - gcj))
            els[c * 8:(c + 1) * 8, :] = jnp.broadcast_to(jnp.exp(g_last), (8, 128))

        q = [qkv[Q_SLAB + h, rows[j], :] for j, h in probs]
        k = [qkv[K_SLAB + h, rows[j], :] for j, h in probs]
        v = [qkv[V_SLAB + h, rows[j], :] for j, h in probs]
        b_col = [beta[j][:, N_HEADS + h:N_HEADS + h + 1] for j, h in probs]
        eg_col = [egc[j][:, h:h + 1] for j, h in probs]
        dec = [jnp.exp(jnp.minimum(gc[j][:, h:h + 1] - gc_t[j][h:h + 1, 0:CG], 0.0)) for j, h in probs]
        n = range(len(probs))
        kb = [k[i] * b_col[i] for i in n]
        m = [_dot_nt(jnp.concatenate([kb[i], q[i]], axis=0).astype(BF16), k[i].astype(BF16)) for i in n]
        lp = [jnp.where(strict_g, m[i][0:CG] * dec[i], 0.0) for i in n]
        for i, (j, h) in enumerate(probs):
            aqs[rows[j], h * D_HEAD:h * D_HEAD + CG] = jnp.where(incl_g, m[i][CG:2 * CG] * dec[i], 0.0)
        p = [-lp[i] for i in n]
        lpb = [lp[i].astype(BF16) for i in n]
        for step in range(n_neumann):
            lp = [_dot(lpb[i], lpb[i]) for i in n]
            lpb = [lp[i].astype(BF16) for i in n]
            p = [p[i] + lp[i] + _dot(p[i].astype(BF16), lpb[i]) for i in n]
        rhs = [jnp.concatenate([v[i] * b_col[i], kb[i] * eg_col[i]], axis=1) for i in n]
        sol = [rhs[i] + _dot(p[i].astype(BF16), rhs[i].astype(BF16)) for i in n]
        for i, (j, h) in enumerate(probs):
            qkv[V_SLAB + h, rows[j], :] = sol[i][:, 0:D_HEAD]
            kcs[rows[j], _hs(0, h)] = sol[i][:, D_HEAD:2 * D_HEAD]
            qkv[Q_SLAB + h, rows[j], :] = q[i] * eg_col[i]
            qkv[K_SLAB + h, rows[j], :] = k[i] * e_to_end[j][:, h:h + 1]

    assert (R // CG) % NB == 0
    for it in range((R // CG) // NB):
        gdn_prepare(it)

    rec = [(bb, h) for bb in seqs for h in heads]

    def gdn_recur(c):
        need("gate_a")
        n = range(len(rec))
        rows = [slice(bb * T + c * CG, bb * T + (c + 1) * CG) for bb, h in rec]
        e_rows = [els[(bb * (T // CG) + c) * 8:(bb * (T // CG) + c) * 8 + 1, :] for bb in seqs]
        e_last = [e_rows[bb][:, h:h + 1] for bb, h in rec]
        s = [sg_ref[bb, h] for bb, h in rec]
        lhs = [jnp.concatenate([kcs[rows[i], _hs(0, h)], qkv[Q_SLAB + h, rows[i], :]], axis=0).astype(BF16)
               for i, (bb, h) in enumerate(rec)]
        qs = [_dot(lhs[i], s[i].astype(BF16)) for i in n]
        fill()
        v_new = [(qkv[V_SLAB + h, rows[i], :] - qs[i][0:CG]).astype(BF16) for i, (bb, h) in enumerate(rec)]
        o = [qs[i][CG:2 * CG] + _dot(aqs[rows[i], h * D_HEAD:h * D_HEAD + CG].astype(BF16), v_new[i])
             for i, (bb, h) in enumerate(rec)]
        upd = [_dot_tn(qkv[K_SLAB + h, rows[i], :].astype(BF16), v_new[i]) for i, (bb, h) in enumerate(rec)]
        fill()
        for i, (bb, h) in enumerate(rec):
            sg_ref[bb, h] = s[i] * e_last[i] + upd[i]
            ms = jnp.mean(o[i] * o[i], axis=-1, keepdims=True)
            gate = pr[rows[i], _hs(R_GA, h)]
            oa_ref[bb, c * CG:(c + 1) * CG, _hs(0, h)] = (
                o[i] * lax.rsqrt(ms + RMS_EPS) * gnw * (gate * _sigmoid(gate))).astype(BF16)

    n_rec = T // CG
    for h in heads:
        hgrn_head(h)
        if h % 2 == 1 and h // 2 < n_rec:
            gdn_recur(h // 2)
    for c in range(N_HEADS // 2, n_rec):
        gdn_recur(c)
    fill(len(pending))

    @pl.when(i == nt - 1)
    def _hgrn_state():
        for bb in seqs:
            for h in heads:
                sh_ref[bb, h] = sht[bb, h].T


def _ffn_kernel(FC, x_ref, oa_ref, ob_ref, mg_ref, wa_ref, wb_ref, wo_ref, g1_ref, b1_ref,
                wgu_ref, wd_ref, g2_ref, b2_ref, y_ref):
    ya = _dot(oa_ref[...], wa_ref[...])
    yb = _dot(ob_ref[...], wb_ref[...])
    merged = (_sigmoid(mg_ref[:, 0:D_MODEL].astype(F32)) * ya
              + _sigmoid(mg_ref[:, D_MODEL:2 * D_MODEL].astype(F32)) * yb)
    mix = _dot(merged.astype(BF16), wo_ref[...])
    x1 = _layer_norm(ALPHA * x_ref[...] + mix, g1_ref[...], b1_ref[...])
    x1b = x1.astype(BF16)
    acc = jnp.zeros(x1.shape, F32)
    for j in range(D_FF // FC):
        gate = _dot(x1b, wgu_ref[:, j * FC:(j + 1) * FC])
        up = _dot(x1b, wgu_ref[:, D_FF + j * FC:D_FF + (j + 1) * FC])
        act = (gate * _sigmoid(gate) * up).astype(BF16)
        acc = acc + _dot(act, wd_ref[j * FC:(j + 1) * FC, :])
    y_ref[...] = _layer_norm(ALPHA * x1 + acc, g2_ref[...], b2_ref[...])


def _const_spec(shape):
    nd = len(shape)
    return pl.BlockSpec(shape, lambda *_: (0,) * nd, pipeline_mode=pl.Buffered(1))


def _tiles(B, L):
    T = min(128, L)
    BB = 2 if (B % 2 == 0 and L > T) else 1
    assert L % T == 0 and L >= CONV_W - 1 and T % 8 == 0
    CG = min(64, T)
    CH = min(16, T)
    assert T % CG == 0 and T % CH == 0 and CG & (CG - 1) == 0 and CH & (CH - 1) == 0 and CH >= 8
    return BB, T, CG, CH


def _layer(layer, x, conv_buf, s_gdn, s_hgrn, w_in_p, conv_w, a_log, dt_bias, gdn_norm_w, lb_logits,
           hgrn_norm_w, w_br_a, w_br_b, w_out, ln1_g, ln1_b, w_gate_up, w_down, ln2_g, ln2_b):
    B, L, _ = x.shape
    BB, T, CG, CH = _tiles(B, L)
    nt = L // T
    R = BB * T
    cache8 = jnp.pad(conv_buf.astype(F32), ((0, 0), (8 - (CONV_W - 1), 0), (0, 0)))
    row128 = lambda v: jnp.pad(v.astype(F32).reshape(1, -1), ((0, 0), (0, 128 - v.shape[-1])))

    seq = lambda w: pl.BlockSpec((BB, T, w), lambda b, i: (b, i, 0))
    per_b = lambda *s: pl.BlockSpec((BB,) + s, lambda b, i: (b,) + (0,) * len(s))
    oa, ob, mg, conv_new, sg_new, sh_new = pl.pallas_call(
        functools.partial(_mixer_kernel, layer, BB, T, CG, CH, nt),
        grid=(B // BB, nt),
        in_specs=[seq(D_MODEL), per_b(8, CONV_CH), per_b(N_HEADS, D_HEAD, D_HEAD), per_b(N_HEADS, D_HEAD, D_HEAD),
                  _const_spec((D_MODEL, P_TOTAL)), _const_spec((CONV_W, CONV_CH)), _const_spec((1, 128)),
                  _const_spec((1, 128)), _const_spec((1, 128)), _const_spec(lb_logits.shape),
                  _const_spec((1, 128))],
        out_specs=[seq(D_MODEL), seq(D_MODEL), seq(2 * D_MODEL), per_b(CONV_W - 1, CONV_CH),
                   per_b(N_HEADS, D_HEAD, D_HEAD), per_b(N_HEADS, D_HEAD, D_HEAD)],
        out_shape=[jax.ShapeDtypeStruct((B, L, D_MODEL), BF16), jax.ShapeDtypeStruct((B, L, D_MODEL), BF16),
                   jax.ShapeDtypeStruct((B, L, 2 * D_MODEL), BF16),
                   jax.ShapeDtypeStruct((B, CONV_W - 1, CONV_CH), F32),
                   jax.ShapeDtypeStruct((B, N_HEADS, D_HEAD, D_HEAD), F32),
                   jax.ShapeDtypeStruct((B, N_HEADS, D_HEAD, D_HEAD), F32)],
        scratch_shapes=[pltpu.VMEM((R, D_MODEL), BF16),
                        pltpu.VMEM((BB, N_SLABS, T + 8, 128), F32),
                        pltpu.VMEM((N_SLABS, R, 128), F32),
                        pltpu.VMEM((R, R_TOTAL), F32),
                        pltpu.VMEM((R, 4 * D_MODEL), F32),
                        pltpu.VMEM((R, D_MODEL), F32),
                        pltpu.VMEM((R, D_MODEL), F32),
                        pltpu.VMEM((8 * (R // CG), 128), F32),
                        pltpu.VMEM((BB, N_HEADS, D_HEAD, D_HEAD), F32)],
        compiler_params=pltpu.CompilerParams(dimension_semantics=("parallel", "arbitrary"),
                                             vmem_limit_bytes=VMEM_LIMIT_BYTES),
        name="mixer",
    )(x, cache8, s_gdn.astype(F32), s_hgrn.astype(F32), w_in_p, conv_w.astype(F32), row128(a_log),
      row128(dt_bias), gdn_norm_w.astype(F32).reshape(1, -1), lb_logits.astype(F32),
      hgrn_norm_w.astype(F32).reshape(1, -1))

    N = B * L
    T2 = min(512, N)
    assert N % T2 == 0
    FC = 1408
    tok = lambda w: pl.BlockSpec((T2, w), lambda i: (i, 0))
    row = lambda v: v.astype(F32).reshape(1, -1)
    y = pl.pallas_call(
        functools.partial(_ffn_kernel, FC),
        grid=(N // T2,),
        in_specs=[tok(D_MODEL), tok(D_MODEL), tok(D_MODEL), tok(2 * D_MODEL),
                  _const_spec((D_MODEL, D_MODEL)), _const_spec((D_MODEL, D_MODEL)), _const_spec((D_MODEL, D_MODEL)),
                  _const_spec((1, D_MODEL)), _const_spec((1, D_MODEL)),
                  _const_spec((D_MODEL, 2 * D_FF)), _const_spec((D_FF, D_MODEL)),
                  _const_spec((1, D_MODEL)), _const_spec((1, D_MODEL))],
        out_specs=tok(D_MODEL),
        out_shape=jax.ShapeDtypeStruct((N, D_MODEL), F32),
        compiler_params=pltpu.CompilerParams(dimension_semantics=("parallel",),
                                             vmem_limit_bytes=VMEM_LIMIT_BYTES),
        name="ffn",
    )(x.reshape(N, D_MODEL), oa.reshape(N, D_MODEL), ob.reshape(N, D_MODEL), mg.reshape(N, 2 * D_MODEL),
      w_br_a, w_br_b, w_out, row(ln1_g), row(ln1_b), w_gate_up, w_down, row(ln2_g), row(ln2_b))
    return y.reshape(B, L, D_MODEL), conv_new, sg_new, sh_new


def _permute_w_in(w):
    n_ab = 2 * N_HEADS
    head = w[:, 0:4 * D_MODEL]
    ab = w[:, 4 * D_MODEL:4 * D_MODEL + n_ab]
    tail = w[:, 4 * D_MODEL + n_ab:]
    pad = jnp.zeros((w.shape[0], 128 - n_ab), w.dtype)
    return jnp.concatenate([head, tail, ab, pad], axis=1).astype(BF16)


def kernel(x_prompt, x_sample, cache_gdn_conv, state_gdn, state_hgrn, w_in, conv_w, a_log, dt_bias, gdn_norm_w, hgrn_lb_logits, hgrn_norm_w, w_br_a, w_br_b, w_out, ln1_g, ln1_b, w_gate_up, w_down, ln2_g, ln2_b):
    depth = w_in.shape[0]
    assert depth == 1, "ALPHA is baked for a single layer"
    B = x_prompt.shape[0]
    dt = x_prompt.dtype
    y_p, y_s = x_prompt, x_sample
    outs = [[] for _ in range(6)]
    for l in range(depth):
        wl = (_permute_w_in(w_in[l]), conv_w[l], a_log[l], dt_bias[l], gdn_norm_w[l], hgrn_lb_logits,
              hgrn_norm_w[l], w_br_a[l].astype(BF16), w_br_b[l].astype(BF16), w_out[l].astype(BF16),
              ln1_g[l], ln1_b[l], w_gate_up[l].astype(BF16), w_down[l].astype(BF16), ln2_g[l], ln2_b[l])
        zc = jnp.zeros((B, CONV_W - 1, CONV_CH), F32)
        zs = jnp.zeros((B, N_HEADS, D_HEAD, D_HEAD), F32)
        y_p, cp, gp, hp = _layer(l, y_p, zc, zs, zs, *wl)
        y_s, cs, gs, hs = _layer(l, y_s, cache_gdn_conv[l], state_gdn[l], state_hgrn[l], *wl)
        for lst, v in zip(outs, (cp, gp, hp, cs, gs, hs)):
            lst.append(v.astype(dt))
    return (y_p.astype(dt), y_s.astype(dt)) + tuple(jnp.stack(o) for o in outs)
```

```python
import functools
import math

import jax
import jax.numpy as jnp
from jax import lax
from jax.experimental import pallas as pl
from jax.experimental.pallas import tpu as pltpu

F32 = jnp.float32
BF16 = jnp.bfloat16

D_MODEL = 1024
N_HEADS = 8
D_HEAD = 128
CONV_W = 4
CONV_CH = 3 * D_MODEL
D_FF = 2816
ALPHA = 2.0 ** 0.25
LN_EPS = 1e-5
RMS_EPS = 1e-6
L2_EPS = 1e-6
EXP_CLAMP = 80.0
GDN_BATCH = 4

P_QKV = 0
P_REST = CONV_CH
P_MG = P_REST + 5 * D_MODEL
P_AB = P_MG + 2 * D_MODEL
P_TOTAL = P_AB + 128
P_QH, P_FH, P_IH, P_GH = (P_REST + s * D_MODEL for s in range(1, 5))
R_GA, R_AB = 0, 1024
R_TOTAL = R_AB + 128
R_QH, R_FH, R_IH, R_GH = 0, 1024, 2048, 3072
PIECE = 256
N_SLABS = CONV_CH // 128
Q_SLAB, K_SLAB, V_SLAB = 0, N_HEADS, 2 * N_HEADS

VMEM_LIMIT_BYTES = 60 * 1024 * 1024


def _dot(a, b):
    return jnp.dot(a, b, preferred_element_type=F32)


def _dot_nt(a, b):
    return lax.dot_general(a, b, (((1,), (1,)), ((), ())), preferred_element_type=F32)


def _dot_tn(a, b):
    return lax.dot_general(a, b, (((0,), (0,)), ((), ())), preferred_element_type=F32)


def _sigmoid(x):
    return 1.0 / (1.0 + jnp.exp(-x))


def _split_bf16(x):
    hi = x.astype(BF16)
    lo = (x - hi.astype(F32)).astype(BF16)
    return hi, lo


def _layer_norm(x, g, b):
    mu = jnp.mean(x, axis=-1, keepdims=True)
    xc = x - mu
    var = jnp.mean(xc * xc, axis=-1, keepdims=True)
    return xc * lax.rsqrt(var + LN_EPS) * g + b


def _hs(off, h):
    return slice(off + h * D_HEAD, off + (h + 1) * D_HEAD)


def _mixer_kernel(layer, BB, T, CG, CH, nt,
                  x_ref, cache_ref, sg0_ref, sh0_ref, w_ref, cw_ref, alog_ref, dtb_ref, gnw_ref,
                  lbl_ref, hnw_ref,
                  oa_ref, ob_ref, mg_ref, conv_ref, sg_ref, sh_ref,
                  xbs, cb, qkv, pr, hp, kcs, aqs, els, sht):
    i = pl.program_id(1)
    heads = range(N_HEADS)
    seqs = range(BB)
    R = BB * T

    @pl.when(i == 0)
    def _init():
        for bb in seqs:
            for blk in range(N_SLABS):
                cb[bb, blk, 0:8, :] = cache_ref[bb, :, blk * 128:(blk + 1) * 128]
            for h in heads:
                sht[bb, h] = sh0_ref[bb, h].T
        sg_ref[...] = sg0_ref[...]

    xbs[...] = x_ref[...].reshape(R, D_MODEL).astype(BF16)

    def project(c0, width):
        return _dot(xbs[...], w_ref[:, c0:c0 + width])

    pending = []

    def add_pieces(tag, c0, width, store):
        for c in range(0, width, PIECE):
            pending.append((tag, lambda c=c: store(c, project(c0 + c, PIECE))))

    def fill(n=1):
        for _ in range(min(n, len(pending))):
            pending.pop(0)[1]()

    def need(tag):
        while any(t == tag for t, _ in pending):
            pending.pop(0)[1]()

    def store_hp(off):
        def store(c, val):
            hp[:, off + c:off + c + PIECE] = val
        return store

    def store_ga(c, val):
        pr[:, R_GA + c:R_GA + c + PIECE] = val

    def store_mg(c, val):
        v16 = val.astype(BF16)
        for bb in seqs:
            mg_ref[bb, :, c:c + PIECE] = v16[bb * T:(bb + 1) * T]

    def add_pair(p):
        for s, c0 in enumerate((P_QH, P_FH, P_IH, P_GH)):
            add_pieces(("pair", p), c0 + p * 2 * D_HEAD, 2 * D_HEAD, store_hp(s * D_MODEL + p * 2 * D_HEAD))

    add_pair(0)
    add_pair(1)
    add_pieces("gate_a", P_REST, D_MODEL, store_ga)
    add_pair(2)
    add_pair(3)
    add_pieces("merge", P_MG, 2 * D_MODEL, store_mg)

    pr[:, R_AB:R_AB + 128] = project(P_AB, 128)
    for s in range(3):
        u = project(P_QKV + s * 1024, 1024)
        for bb in seqs:
            for j in heads:
                cb[bb, s * N_HEADS + j, 8:8 + T, :] = u[bb * T:(bb + 1) * T, j * 128:(j + 1) * 128]

    for blk in range(N_SLABS):
        cs = slice(blk * 128, (blk + 1) * 128)
        wj = [cw_ref[j:j + 1, cs] for j in range(CONV_W)]
        for bb in seqs:
            taps = {m: cb[bb, blk, pl.ds(m, T // 8, stride=8), :] for m in range(5, 16)}
            for r in range(8):
                y = taps[r + 5] * wj[0] + taps[r + 6] * wj[1] + taps[r + 7] * wj[2] + taps[r + 8] * wj[3]
                y = y * _sigmoid(y)
                if blk < 2 * N_HEADS:
                    y = y * lax.rsqrt(jnp.sum(y * y, axis=-1, keepdims=True) + L2_EPS)
                    if blk < N_HEADS:
                        y = y * (D_HEAD ** -0.5)
                qkv[blk, pl.ds(bb * T + r, T // 8, stride=8), :] = y
        if blk >= N_HEADS and blk % 2 == 1:
            fill()

    @pl.when(i == nt - 1)
    def _conv_state():
        for bb in seqs:
            for blk in range(N_SLABS):
                conv_ref[bb, :, blk * 128:(blk + 1) * 128] = cb[bb, blk, T + 5:T + 8, :]

    cb[:, :, 0:8, :] = cb[:, :, T:T + 8, :]

    ll = lbl_ref[...]
    el = jnp.exp(ll - jnp.max(ll, axis=0, keepdims=True))
    lb = jnp.sum(el[0:layer + 1], axis=0, keepdims=True) / jnp.sum(el, axis=0, keepdims=True)
    hnw = hnw_ref[...]
    row_t = lax.broadcasted_iota(jnp.int32, (T, T), 0)
    col_t = lax.broadcasted_iota(jnp.int32, (T, T), 1)
    xor_t = row_t ^ col_t
    before_t = col_t < row_t
    causal_t = row_t >= col_t
    near_t = causal_t & (xor_t < CH)
    tril_t = jnp.concatenate([causal_t.astype(BF16)] * 2, axis=1)
    widths = [w for w in (2 ** e for e in range(1, 16)) if CH < w <= T]
    level_mask = [before_t & (xor_t >= w // 2) & (xor_t < w) for w in widths]

    def block_row(x, w, r):
        x3 = x.reshape(T // w, w, D_HEAD)
        return jnp.broadcast_to(x3[:, r:r + 1, :], x3.shape).reshape(T, D_HEAD)

    def hgrn_pair(p):
        need(("pair", p))
        probs_h = [(h, bb) for h in (2 * p, 2 * p + 1) for bb in seqs]
        n = range(len(probs_h))
        rows = [slice(bb * T, (bb + 1) * T) for h, bb in probs_h]
        lbh = [lb[:, _hs(0, h)] for h, bb in probs_h]
        z = [hp[rows[i], _hs(R_FH, h)] for i, (h, bb) in enumerate(probs_h)]
        e = [jnp.exp(-jnp.abs(z[i])) for i in n]
        r = [1.0 / (1.0 + e[i]) for i in n]
        sig_pos = [jnp.where(z[i] >= 0.0, r[i], e[i] * r[i]) for i in n]
        sig_neg = [jnp.where(z[i] >= 0.0, e[i] * r[i], r[i]) for i in n]
        logf = [jnp.log(lbh[i] + (1.0 - lbh[i]) * sig_pos[i]) for i in n]
        k = [(1.0 - lbh[i]) * sig_neg[i] for i in n]
        qh = [hp[rows[i], _hs(R_QH, h)] for i, (h, bb) in enumerate(probs_h)]
        q = [qh[i] * _sigmoid(qh[i]) * (D_HEAD ** -0.5) for i in n]
        vb = [hp[rows[i], _hs(R_IH, h)].astype(BF16) for i, (h, bb) in enumerate(probs_h)]
        split = [_split_bf16(logf[i]) for i in n]
        b = [_dot(tril_t, jnp.concatenate(split[i], axis=0)) for i in n]
        fill(2)
        bm = [b[i] - block_row(b[i], CH, CH // 2 - 1) for i in n]
        q_in = [(q[i] * jnp.exp(jnp.minimum(bm[i], EXP_CLAMP))).astype(BF16) for i in n]
        k_in = [(k[i] * jnp.exp(jnp.minimum(-bm[i], EXP_CLAMP))).astype(BF16) for i in n]
        a = [jnp.where(near_t, _dot_nt(q_in[i], k_in[i]), 0.0) for i in n]
        for w, mask in zip(widths, level_mask):
            hw = w // 2
            q_w, k_w = [], []
            for i in n:
                b3, q3, k3 = (x.reshape(T // w, w, D_HEAD) for x in (b[i], q[i], k[i]))
                ref = b3[:, hw - 1:hw, :]
                q_up = (q3[:, hw:, :] * jnp.exp(b3[:, hw:, :] - ref)).astype(BF16)
                k_lo = (k3[:, :hw, :] * jnp.exp(ref - b3[:, :hw, :])).astype(BF16)
                zero = jnp.zeros((T // w, hw, D_HEAD), BF16)
                q_w.append(jnp.concatenate([zero, q_up], axis=1).reshape(T, D_HEAD))
                k_w.append(jnp.concatenate([k_lo, zero], axis=1).reshape(T, D_HEAD))
            a = [jnp.where(mask, _dot_nt(q_w[i], k_w[i]), a[i]) for i in n]
        fill(2)
        b_end = [b[i][T - 1:T, :] for i in n]
        st = [sht[bb, h] for h, bb in probs_h]
        q_st = [(q[i] * jnp.exp(b[i])).astype(BF16) for i in n]
        k_st = [(k[i] * jnp.exp(b_end[i] - b[i])).astype(BF16) for i in n]
        o = [_dot(a[i].astype(BF16), vb[i]) + _dot_nt(q_st[i], st[i].astype(BF16)) for i in n]
        upd = [_dot_tn(vb[i], k_st[i]) for i in n]
        for i, (h, bb) in enumerate(probs_h):
            sht[bb, h] = st[i] * jnp.exp(b_end[i]) + upd[i]
            ms = jnp.mean(o[i] * o[i], axis=-1, keepdims=True)
            gate = hp[rows[i], _hs(R_GH, h)]
            ob_ref[bb, :, _hs(0, h)] = (o[i] * lax.rsqrt(ms + RMS_EPS) * hnw * _sigmoid(gate)).astype(BF16)

    neg_a = -jnp.exp(alog_ref[...])
    dtb = dtb_ref[...]
    gnw = gnw_ref[...]
    lane_g = lax.broadcasted_iota(jnp.int32, (CG, 128), 1)
    row_g = lax.broadcasted_iota(jnp.int32, (CG, CG), 0)
    col_g = lax.broadcasted_iota(jnp.int32, (CG, CG), 1)
    incl_g = row_g >= col_g
    strict_g = row_g > col_g
    tril_g = jnp.concatenate([incl_g.astype(BF16)] * 2, axis=1)
    n_neumann = int(math.log2(CG)) - 1

    NB = min(GDN_BATCH, R // CG)
    probs = [(j, h) for j in range(NB) for h in heads]

    def gdn_prepare(it):
        rows, gc, gc_t, egc, e_to_end, beta = [], [], [], [], [], []
        for j in range(NB):
            c = it * NB + j
            rows.append(slice(c * CG, (c + 1) * CG))
            ab = pr[rows[j], R_AB:R_AB + 128]
            xa = ab + dtb
            softplus = jnp.maximum(xa, 0.0) + jnp.log(1.0 + jnp.exp(-jnp.abs(xa)))
            g = jnp.where(lane_g < N_HEADS, neg_a * softplus, 0.0)
            beta.append(_sigmoid(ab))
            gcj = _dot(tril_g, jnp.concatenate(_split_bf16(g), axis=0))
            gc.append(gcj)
            gc_sq = jnp.concatenate([gcj, jnp.zeros((128 - CG, 128), F32)], axis=0) if CG < 128 else gcj
            gc_t.append(gc_sq.T)
            egc.append(jnp.exp(gcj))
            g_last = gcj[CG - 1:CG, :]
            e_to_end.append(jnp.exp(g_last - gcj))
            els[c * 8:(c + 1) * 8, :] = jnp.broadcast_to(jnp.exp(g_last), (8, 128))

        q = [qkv[Q_SLAB + h, rows[j], :] for j, h in probs]
        k = [qkv[K_SLAB + h, rows[j], :] for j, h in probs]
        v = [qkv[V_SLAB + h, rows[j], :] for j, h in probs]
        b_col = [beta[j][:, N_HEADS + h:N_HEADS + h + 1] for j, h in probs]
        eg_col = [egc[j][:, h:h + 1] for j, h in probs]
        dec = [jnp.exp(jnp.minimum(gc[j][:, h:h + 1] - gc_t[j][h:h + 1, 0:CG], 0.0)) for j, h in probs]
        n = range(len(probs))
        kb = [k[i] * b_col[i] for i in n]
        m = [_dot_nt(jnp.concatenate([kb[i], q[i]], axis=0).astype(BF16), k[i].astype(BF16)) for i in n]
        lp = [jnp.where(strict_g, m[i][0:CG] * dec[i], 0.0) for i in n]
        for i, (j, h) in enumerate(probs):
            aqs[rows[j], h * D_HEAD:h * D_HEAD + CG] = jnp.where(incl_g, m[i][CG:2 * CG] * dec[i], 0.0)
        p = [-lp[i] for i in n]
        lpb = [lp[i].astype(BF16) for i in n]
        for step in range(n_neumann):
            lp = [_dot(lpb[i], lpb[i]) for i in n]
            lpb = [lp[i].astype(BF16) for i in n]
            p = [p[i] + lp[i] + _dot(p[i].astype(BF16), lpb[i]) for i in n]
        rhs = [jnp.concatenate([v[i] * b_col[i], kb[i] * eg_col[i]], axis=1) for i in n]
        sol = [rhs[i] + _dot(p[i].astype(BF16), rhs[i].astype(BF16)) for i in n]
        for i, (j, h) in enumerate(probs):
            qkv[V_SLAB + h, rows[j], :] = sol[i][:, 0:D_HEAD]
            kcs[rows[j], _hs(0, h)] = sol[i][:, D_HEAD:2 * D_HEAD]
            qkv[Q_SLAB + h, rows[j], :] = q[i] * eg_col[i]
            qkv[K_SLAB + h, rows[j], :] = k[i] * e_to_end[j][:, h:h + 1]

    assert (R // CG) % NB == 0
    for it in range((R // CG) // NB):
        gdn_prepare(it)

    rec = [(bb, h) for bb in seqs for h in heads]

    def gdn_recur(c):
        need("gate_a")
        n = range(len(rec))
        rows = [slice(bb * T + c * CG, bb * T + (c + 1) * CG) for bb, h in rec]
        e_rows = [els[(bb * (T // CG) + c) * 8:(bb * (T // CG) + c) * 8 + 1, :] for bb in seqs]
        e_last = [e_rows[bb][:, h:h + 1] for bb, h in rec]
        s = [sg_ref[bb, h] for bb, h in rec]
        lhs = [jnp.concatenate([kcs[rows[i], _hs(0, h)], qkv[Q_SLAB + h, rows[i], :]], axis=0).astype(BF16)
               for i, (bb, h) in enumerate(rec)]
        qs = [_dot(lhs[i], s[i].astype(BF16)) for i in n]
        fill()
        v_new = [(qkv[V_SLAB + h, rows[i], :] - qs[i][0:CG]).astype(BF16) for i, (bb, h) in enumerate(rec)]
        o = [qs[i][CG:2 * CG] + _dot(aqs[rows[i], h * D_HEAD:h * D_HEAD + CG].astype(BF16), v_new[i])
             for i, (bb, h) in enumerate(rec)]
        upd = [_dot_tn(qkv[K_SLAB + h, rows[i], :].astype(BF16), v_new[i]) for i, (bb, h) in enumerate(rec)]
        fill()
        for i, (bb, h) in enumerate(rec):
            sg_ref[bb, h] = s[i] * e_last[i] + upd[i]
            ms = jnp.mean(o[i] * o[i], axis=-1, keepdims=True)
            gate = pr[rows[i], _hs(R_GA, h)]
            oa_ref[bb, c * CG:(c + 1) * CG, _hs(0, h)] = (
                o[i] * lax.rsqrt(ms + RMS_EPS) * gnw * (gate * _sigmoid(gate))).astype(BF16)

    n_rec = T // CG
    for p in range(N_HEADS // 2):
        hgrn_pair(p)
        if p < n_rec:
            gdn_recur(p)
    for c in range(N_HEADS // 2, n_rec):
        gdn_recur(c)
    fill(len(pending))

    @pl.when(i == nt - 1)
    def _hgrn_state():
        for bb in seqs:
            for h in heads:
                sh_ref[bb, h] = sht[bb, h].T


def _ffn_kernel(FC, NP, x_ref, oa_ref, ob_ref, mg_ref, wa_ref, wb_ref, wo_ref, g1_ref, b1_ref,
                wgu_ref, wd_ref, g2_ref, b2_ref, y_ref):
    rows_per = x_ref.shape[0] // NP
    parts = range(NP)
    rows = [slice(p * rows_per, (p + 1) * rows_per) for p in parts]
    ya = [_dot(oa_ref[rows[p], :], wa_ref[...]) for p in parts]
    yb = [_dot(ob_ref[rows[p], :], wb_ref[...]) for p in parts]
    merged = [(_sigmoid(mg_ref[rows[p], 0:D_MODEL].astype(F32)) * ya[p]
               + _sigmoid(mg_ref[rows[p], D_MODEL:2 * D_MODEL].astype(F32)) * yb[p]).astype(BF16) for p in parts]
    mix = [_dot(merged[p], wo_ref[...]) for p in parts]
    x1 = [_layer_norm(ALPHA * x_ref[rows[p], :] + mix[p], g1_ref[...], b1_ref[...]) for p in parts]
    x1b = [x1[p].astype(BF16) for p in parts]
    acc = [jnp.zeros(x1[p].shape, F32) for p in parts]
    for j in range(D_FF // FC):
        gate = [_dot(x1b[p], wgu_ref[:, j * FC:(j + 1) * FC]) for p in parts]
        up = [_dot(x1b[p], wgu_ref[:, D_FF + j * FC:D_FF + (j + 1) * FC]) for p in parts]
        act = [(gate[p] * _sigmoid(gate[p]) * up[p]).astype(BF16) for p in parts]
        acc = [acc[p] + _dot(act[p], wd_ref[j * FC:(j + 1) * FC, :]) for p in parts]
    for p in parts:
        y_ref[rows[p], :] = _layer_norm(ALPHA * x1[p] + acc[p], g2_ref[...], b2_ref[...])


def _const_spec(shape):
    nd = len(shape)
    return pl.BlockSpec(shape, lambda *_: (0,) * nd, pipeline_mode=pl.Buffered(1))


def _tiles(B, L):
    T = min(128, L)
    BB = 2 if (B % 2 == 0 and L > T) else 1
    assert L % T == 0 and L >= CONV_W - 1 and T % 8 == 0
    CG = min(64, T)
    CH = min(16, T)
    assert T % CG == 0 and T % CH == 0 and CG & (CG - 1) == 0 and CH & (CH - 1) == 0 and CH >= 8
    return BB, T, CG, CH


def _layer(layer, x, conv_buf, s_gdn, s_hgrn, w_in_p, conv_w, a_log, dt_bias, gdn_norm_w, lb_logits,
           hgrn_norm_w, w_br_a, w_br_b, w_out, ln1_g, ln1_b, w_gate_up, w_down, ln2_g, ln2_b):
    B, L, _ = x.shape
    BB, T, CG, CH = _tiles(B, L)
    nt = L // T
    R = BB * T
    cache8 = jnp.pad(conv_buf.astype(F32), ((0, 0), (8 - (CONV_W - 1), 0), (0, 0)))
    row128 = lambda v: jnp.pad(v.astype(F32).reshape(1, -1), ((0, 0), (0, 128 - v.shape[-1])))

    seq = lambda w: pl.BlockSpec((BB, T, w), lambda b, i: (b, i, 0))
    per_b = lambda *s: pl.BlockSpec((BB,) + s, lambda b, i: (b,) + (0,) * len(s))
    oa, ob, mg, conv_new, sg_new, sh_new = pl.pallas_call(
        functools.partial(_mixer_kernel, layer, BB, T, CG, CH, nt),
        grid=(B // BB, nt),
        in_specs=[seq(D_MODEL), per_b(8, CONV_CH), per_b(N_HEADS, D_HEAD, D_HEAD), per_b(N_HEADS, D_HEAD, D_HEAD),
                  _const_spec((D_MODEL, P_TOTAL)), _const_spec((CONV_W, CONV_CH)), _const_spec((1, 128)),
                  _const_spec((1, 128)), _const_spec((1, 128)), _const_spec(lb_logits.shape),
                  _const_spec((1, 128))],
        out_specs=[seq(D_MODEL), seq(D_MODEL), seq(2 * D_MODEL), per_b(CONV_W - 1, CONV_CH),
                   per_b(N_HEADS, D_HEAD, D_HEAD), per_b(N_HEADS, D_HEAD, D_HEAD)],
        out_shape=[jax.ShapeDtypeStruct((B, L, D_MODEL), BF16), jax.ShapeDtypeStruct((B, L, D_MODEL), BF16),
                   jax.ShapeDtypeStruct((B, L, 2 * D_MODEL), BF16),
                   jax.ShapeDtypeStruct((B, CONV_W - 1, CONV_CH), F32),
                   jax.ShapeDtypeStruct((B, N_HEADS, D_HEAD, D_HEAD), F32),
                   jax.ShapeDtypeStruct((B, N_HEADS, D_HEAD, D_HEAD), F32)],
        scratch_shapes=[pltpu.VMEM((R, D_MODEL), BF16),
                        pltpu.VMEM((BB, N_SLABS, T + 8, 128), F32),
                        pltpu.VMEM((N_SLABS, R, 128), F32),
                        pltpu.VMEM((R, R_TOTAL), F32),
                        pltpu.VMEM((R, 4 * D_MODEL), F32),
                        pltpu.VMEM((R, D_MODEL), F32),
                        pltpu.VMEM((R, D_MODEL), F32),
                        pltpu.VMEM((8 * (R // CG), 128), F32),
                        pltpu.VMEM((BB, N_HEADS, D_HEAD, D_HEAD), F32)],
        compiler_params=pltpu.CompilerParams(dimension_semantics=("parallel", "arbitrary"),
                                             vmem_limit_bytes=VMEM_LIMIT_BYTES),
        name="mixer",
    )(x, cache8, s_gdn.astype(F32), s_hgrn.astype(F32), w_in_p, conv_w.astype(F32), row128(a_log),
      row128(dt_bias), gdn_norm_w.astype(F32).reshape(1, -1), lb_logits.astype(F32),
      hgrn_norm_w.astype(F32).reshape(1, -1))

    N = B * L
    T2 = min(512, N)
    assert N % T2 == 0
    FC = 1408
    NP = 2 if T2 % 512 == 0 else 1
    tok = lambda w: pl.BlockSpec((T2, w), lambda i: (i, 0))
    row = lambda v: v.astype(F32).reshape(1, -1)
    y = pl.pallas_call(
        functools.partial(_ffn_kernel, FC, NP),
        grid=(N // T2,),
        in_specs=[tok(D_MODEL), tok(D_MODEL), tok(D_MODEL), tok(2 * D_MODEL),
                  _const_spec((D_MODEL, D_MODEL)), _const_spec((D_MODEL, D_MODEL)), _const_spec((D_MODEL, D_MODEL)),
                  _const_spec((1, D_MODEL)), _const_spec((1, D_MODEL)),
                  _const_spec((D_MODEL, 2 * D_FF)), _const_spec((D_FF, D_MODEL)),
                  _const_spec((1, D_MODEL)), _const_spec((1, D_MODEL))],
        out_specs=tok(D_MODEL),
        out_shape=jax.ShapeDtypeStruct((N, D_MODEL), F32),
        compiler_params=pltpu.CompilerParams(dimension_semantics=("parallel",),
                                             vmem_limit_bytes=VMEM_LIMIT_BYTES),
        name="ffn",
    )(x.reshape(N, D_MODEL), oa.reshape(N, D_MODEL), ob.reshape(N, D_MODEL), mg.reshape(N, 2 * D_MODEL),
      w_br_a, w_br_b, w_out, row(ln1_g), row(ln1_b), w_gate_up, w_down, row(ln2_g), row(ln2_b))
    return y.reshape(B, L, D_MODEL), conv_new, sg_new, sh_new


def _permute_w_in(w):
    n_ab = 2 * N_HEADS
    w = w.astype(BF16)
    head = w[:, 0:4 * D_MODEL]
    ab = w[:, 4 * D_MODEL:4 * D_MODEL + n_ab]
    tail = w[:, 4 * D_MODEL + n_ab:]
    pad = jnp.zeros((w.shape[0], 128 - n_ab), w.dtype)
    return jnp.concatenate([head, tail, ab, pad], axis=1)


def kernel(x_prompt, x_sample, cache_gdn_conv, state_gdn, state_hgrn, w_in, conv_w, a_log, dt_bias, gdn_norm_w, hgrn_lb_logits, hgrn_norm_w, w_br_a, w_br_b, w_out, ln1_g, ln1_b, w_gate_up, w_down, ln2_g, ln2_b):
    depth = w_in.shape[0]
    assert depth == 1, "ALPHA is baked for a single layer"
    B = x_prompt.shape[0]
    dt = x_prompt.dtype
    y_p, y_s = x_prompt, x_sample
    outs = [[] for _ in range(6)]
    for l in range(depth):
        wl = (_permute_w_in(w_in[l]), conv_w[l], a_log[l], dt_bias[l], gdn_norm_w[l], hgrn_lb_logits,
              hgrn_norm_w[l], w_br_a[l].astype(BF16), w_br_b[l].astype(BF16), w_out[l].astype(BF16),
              ln1_g[l], ln1_b[l], w_gate_up[l].astype(BF16), w_down[l].astype(BF16), ln2_g[l], ln2_b[l])
        zc = jnp.zeros((B, CONV_W - 1, CONV_CH), F32)
        zs = jnp.zeros((B, N_HEADS, D_HEAD, D_HEAD), F32)
        y_p, cp, gp, hp = _layer(l, y_p, zc, zs, zs, *wl)
        y_s, cs, gs, hs = _layer(l, y_s, cache_gdn_conv[l], state_gdn[l], state_hgrn[l], *wl)
        for lst, v in zip(outs, (cp, gp, hp, cs, gs, hs)):
            lst.append(v.astype(dt))
    return (y_p.astype(dt), y_s.astype(dt)) + tuple(jnp.stack(o) for o in outs)
```

```python
import functools
import math

import jax
import jax.numpy as jnp
from jax import lax
from jax.experimental import pallas as pl
from jax.experimental.pallas import tpu as pltpu

F32 = jnp.float32
BF16 = jnp.bfloat16

D_MODEL = 1024
N_HEADS = 8
D_HEAD = 128
CONV_W = 4
CONV_CH = 3 * D_MODEL
D_FF = 2816
ALPHA = 2.0 ** 0.25
LN_EPS = 1e-5
RMS_EPS = 1e-6
L2_EPS = 1e-6
EXP_CLAMP = 80.0
GDN_BATCH = 4

P_QKV = 0
P_REST = CONV_CH
P_MG = P_REST + 5 * D_MODEL
P_AB = P_MG + 2 * D_MODEL
P_TOTAL = P_AB + 128
P_QH, P_FH, P_IH, P_GH = (P_REST + s * D_MODEL for s in range(1, 5))
R_GA, R_AB = 0, 1024
R_TOTAL = R_AB + 128
R_QH, R_FH, R_IH, R_GH = 0, 1024, 2048, 3072
PIECE = 256
N_SLABS = CONV_CH // 128
Q_SLAB, K_SLAB, V_SLAB = 0, N_HEADS, 2 * N_HEADS

VMEM_LIMIT_BYTES = 60 * 1024 * 1024


def _dot(a, b):
    return jnp.dot(a, b, preferred_element_type=F32)


def _dot_nt(a, b):
    return lax.dot_general(a, b, (((1,), (1,)), ((), ())), preferred_element_type=F32)


def _dot_tn(a, b):
    return lax.dot_general(a, b, (((0,), (0,)), ((), ())), preferred_element_type=F32)


def _sigmoid(x):
    return 1.0 / (1.0 + jnp.exp(-x))


def _split_bf16(x):
    hi = x.astype(BF16)
    lo = (x - hi.astype(F32)).astype(BF16)
    return hi, lo


def _layer_norm(x, g, b):
    mu = jnp.mean(x, axis=-1, keepdims=True)
    xc = x - mu
    var = jnp.mean(xc * xc, axis=-1, keepdims=True)
    return xc * lax.rsqrt(var + LN_EPS) * g + b


def _hs(off, h):
    return slice(off + h * D_HEAD, off + (h + 1) * D_HEAD)


def _mixer_kernel(layer, BB, T, CG, CH, nt,
                  x_ref, cache_ref, sg0_ref, sh0_ref, w_ref, cw_ref, alog_ref, dtb_ref, gnw_ref,
                  lbl_ref, hnw_ref,
                  oa_ref, ob_ref, mg_ref, conv_ref, sg_ref, sh_ref,
                  xbs, cb, qkv, pr, hp, kcs, aqs, els, sht):
    i = pl.program_id(1)
    heads = range(N_HEADS)
    seqs = range(BB)
    R = BB * T

    @pl.when(i == 0)
    def _init():
        for bb in seqs:
            for blk in range(N_SLABS):
                cb[bb, blk, 0:8, :] = cache_ref[bb, :, blk * 128:(blk + 1) * 128]
            for h in heads:
                sht[bb, h] = sh0_ref[bb, h].T
        sg_ref[...] = sg0_ref[...]

    xbs[...] = x_ref[...].reshape(R, D_MODEL).astype(BF16)

    def project(c0, width):
        return _dot(xbs[...], w_ref[:, c0:c0 + width])

    pending = []

    def add_pieces(tag, c0, width, store):
        for c in range(0, width, PIECE):
            pending.append((tag, lambda c=c: store(c, project(c0 + c, PIECE))))

    def fill(n=1):
        for _ in range(min(n, len(pending))):
            pending.pop(0)[1]()

    def need(tag):
        while any(t == tag for t, _ in pending):
            pending.pop(0)[1]()

    def store_hp(off):
        def store(c, val):
            hp[:, off + c:off + c + PIECE] = val
        return store

    def store_ga(c, val):
        pr[:, R_GA + c:R_GA + c + PIECE] = val

    def store_mg(c, val):
        v16 = val.astype(BF16)
        for bb in seqs:
            mg_ref[bb, :, c:c + PIECE] = v16[bb * T:(bb + 1) * T]

    def add_pair(p):
        for s, c0 in enumerate((P_QH, P_FH, P_IH, P_GH)):
            add_pieces(("pair", p), c0 + p * 2 * D_HEAD, 2 * D_HEAD, store_hp(s * D_MODEL + p * 2 * D_HEAD))

    add_pair(0)
    add_pair(1)
    add_pieces("gate_a", P_REST, D_MODEL, store_ga)
    add_pair(2)
    add_pair(3)
    add_pieces("merge", P_MG, 2 * D_MODEL, store_mg)

    pr[:, R_AB:R_AB + 128] = project(P_AB, 128)
    for s in range(3):
        u = project(P_QKV + s * 1024, 1024)
        for bb in seqs:
            for j in heads:
                cb[bb, s * N_HEADS + j, 8:8 + T, :] = u[bb * T:(bb + 1) * T, j * 128:(j + 1) * 128]

    for blk in range(N_SLABS):
        cs = slice(blk * 128, (blk + 1) * 128)
        wj = [cw_ref[j:j + 1, cs] for j in range(CONV_W)]
        for bb in seqs:
            taps = {m: cb[bb, blk, pl.ds(m, T // 8, stride=8), :] for m in range(5, 16)}
            for r in range(8):
                y = taps[r + 5] * wj[0] + taps[r + 6] * wj[1] + taps[r + 7] * wj[2] + taps[r + 8] * wj[3]
                y = y * _sigmoid(y)
                if blk < 2 * N_HEADS:
                    y = y * lax.rsqrt(jnp.sum(y * y, axis=-1, keepdims=True) + L2_EPS)
                    if blk < N_HEADS:
                        y = y * (D_HEAD ** -0.5)
                qkv[blk, pl.ds(bb * T + r, T // 8, stride=8), :] = y
        if blk >= N_HEADS and blk % 4 == 3:
            fill()

    @pl.when(i == nt - 1)
    def _conv_state():
        for bb in seqs:
            for blk in range(N_SLABS):
                conv_ref[bb, :, blk * 128:(blk + 1) * 128] = cb[bb, blk, T + 5:T + 8, :]

    cb[:, :, 0:8, :] = cb[:, :, T:T + 8, :]

    ll = lbl_ref[...]
    el = jnp.exp(ll - jnp.max(ll, axis=0, keepdims=True))
    lb = jnp.sum(el[0:layer + 1], axis=0, keepdims=True) / jnp.sum(el, axis=0, keepdims=True)
    hnw = hnw_ref[...]
    row_t = lax.broadcasted_iota(jnp.int32, (T, T), 0)
    col_t = lax.broadcasted_iota(jnp.int32, (T, T), 1)
    xor_t = row_t ^ col_t
    before_t = col_t < row_t
    causal_t = row_t >= col_t
    near_t = causal_t & (xor_t < CH)
    tril_t = jnp.concatenate([causal_t.astype(BF16)] * 2, axis=1)
    widths = [w for w in (2 ** e for e in range(1, 16)) if CH < w <= T]
    level_mask = [before_t & (xor_t >= w // 2) & (xor_t < w) for w in widths]

    def block_row(x, w, r):
        x3 = x.reshape(T // w, w, D_HEAD)
        return jnp.broadcast_to(x3[:, r:r + 1, :], x3.shape).reshape(T, D_HEAD)

    def hgrn_pair(p):
        need(("pair", p))
        probs_h = [(h, bb) for h in (2 * p, 2 * p + 1) for bb in seqs]
        n = range(len(probs_h))
        rows = [slice(bb * T, (bb + 1) * T) for h, bb in probs_h]
        lbh = [lb[:, _hs(0, h)] for h, bb in probs_h]
        z = [hp[rows[i], _hs(R_FH, h)] for i, (h, bb) in enumerate(probs_h)]
        e = [jnp.exp(-jnp.abs(z[i])) for i in n]
        r = [1.0 / (1.0 + e[i]) for i in n]
        sig_pos = [jnp.where(z[i] >= 0.0, r[i], e[i] * r[i]) for i in n]
        sig_neg = [jnp.where(z[i] >= 0.0, e[i] * r[i], r[i]) for i in n]
        logf = [jnp.log(lbh[i] + (1.0 - lbh[i]) * sig_pos[i]) for i in n]
        k = [(1.0 - lbh[i]) * sig_neg[i] for i in n]
        qh = [hp[rows[i], _hs(R_QH, h)] for i, (h, bb) in enumerate(probs_h)]
        q = [qh[i] * _sigmoid(qh[i]) * (D_HEAD ** -0.5) for i in n]
        vb = [hp[rows[i], _hs(R_IH, h)].astype(BF16) for i, (h, bb) in enumerate(probs_h)]
        split = [_split_bf16(logf[i]) for i in n]
        b = [_dot(tril_t, jnp.concatenate(split[i], axis=0)) for i in n]
        fill(2)
        bm = [b[i] - block_row(b[i], CH, CH // 2 - 1) for i in n]
        q_in = [(q[i] * jnp.exp(jnp.minimum(bm[i], EXP_CLAMP))).astype(BF16) for i in n]
        k_in = [(k[i] * jnp.exp(jnp.minimum(-bm[i], EXP_CLAMP))).astype(BF16) for i in n]
        a = [jnp.where(near_t, _dot_nt(q_in[i], k_in[i]), 0.0) for i in n]
        for w, mask in zip(widths, level_mask):
            hw = w // 2
            q_w, k_w = [], []
            for i in n:
                b3, q3, k3 = (x.reshape(T // w, w, D_HEAD) for x in (b[i], q[i], k[i]))
                ref = b3[:, hw - 1:hw, :]
                q_up = (q3[:, hw:, :] * jnp.exp(b3[:, hw:, :] - ref)).astype(BF16)
                k_lo = (k3[:, :hw, :] * jnp.exp(ref - b3[:, :hw, :])).astype(BF16)
                zero = jnp.zeros((T // w, hw, D_HEAD), BF16)
                q_w.append(jnp.concatenate([zero, q_up], axis=1).reshape(T, D_HEAD))
                k_w.append(jnp.concatenate([k_lo, zero], axis=1).reshape(T, D_HEAD))
            a = [jnp.where(mask, _dot_nt(q_w[i], k_w[i]), a[i]) for i in n]
        fill(2)
        b_end = [b[i][T - 1:T, :] for i in n]
        st = [sht[bb, h] for h, bb in probs_h]
        q_st = [(q[i] * jnp.exp(b[i])).astype(BF16) for i in n]
        k_st = [(k[i] * jnp.exp(b_end[i] - b[i])).astype(BF16) for i in n]
        o = [_dot(a[i].astype(BF16), vb[i]) + _dot_nt(q_st[i], st[i].astype(BF16)) for i in n]
        upd = [_dot_tn(vb[i], k_st[i]) for i in n]
        for i, (h, bb) in enumerate(probs_h):
            sht[bb, h] = st[i] * jnp.exp(b_end[i]) + upd[i]
            ms = jnp.mean(o[i] * o[i], axis=-1, keepdims=True)
            gate = hp[rows[i], _hs(R_GH, h)]
            ob_ref[bb, :, _hs(0, h)] = (o[i] * lax.rsqrt(ms + RMS_EPS) * hnw * _sigmoid(gate)).astype(BF16)

    neg_a = -jnp.exp(alog_ref[...])
    dtb = dtb_ref[...]
    gnw = gnw_ref[...]
    lane_g = lax.broadcasted_iota(jnp.int32, (CG, 128), 1)
    row_g = lax.broadcasted_iota(jnp.int32, (CG, CG), 0)
    col_g = lax.broadcasted_iota(jnp.int32, (CG, CG), 1)
    incl_g = row_g >= col_g
    strict_g = row_g > col_g
    tril_g = jnp.concatenate([incl_g.astype(BF16)] * 2, axis=1)
    n_neumann = int(math.log2(CG)) - 1

    NB = min(GDN_BATCH, R // CG)
    probs = [(j, h) for j in range(NB) for h in heads]

    def gdn_prepare(it):
        rows, gc, gc_t, egc, e_to_end, beta = [], [], [], [], [], []
        for j in range(NB):
            c = it * NB + j
            rows.append(slice(c * CG, (c + 1) * CG))
            ab = pr[rows[j], R_AB:R_AB + 128]
            xa = ab + dtb
            softplus = jnp.maximum(xa, 0.0) + jnp.log(1.0 + jnp.exp(-jnp.abs(xa)))
            g = jnp.where(lane_g < N_HEADS, neg_a * softplus, 0.0)
            beta.append(_sigmoid(ab))
            gcj = _dot(tril_g, jnp.concatenate(_split_bf16(g), axis=0))
            gc.append(gcj)
            gc_sq = jnp.concatenate([gcj, jnp.zeros((128 - CG, 128), F32)], axis=0) if CG < 128 else gcj
            gc_t.append(gc_sq.T)
            egc.append(jnp.exp(gcj))
            g_last = gcj[CG - 1:CG, :]
            e_to_end.append(jnp.exp(g_last - gcj))
            els[c * 8:(c + 1) * 8, :] = jnp.broadcast_to(jnp.exp(g_last), (8, 128))

        q = [qkv[Q_SLAB + h, rows[j], :] for j, h in probs]
        k = [qkv[K_SLAB + h, rows[j], :] for j, h in probs]
        v = [qkv[V_SLAB + h, rows[j], :] for j, h in probs]
        b_col = [beta[j][:, N_HEADS + h:N_HEADS + h + 1] for j, h in probs]
        eg_col = [egc[j][:, h:h + 1] for j, h in probs]
        dec = [jnp.exp(jnp.minimum(gc[j][:, h:h + 1] - gc_t[j][h:h + 1, 0:CG], 0.0)) for j, h in probs]
        n = range(len(probs))
        kb = [k[i] * b_col[i] for i in n]
        m = [_dot_nt(jnp.concatenate([kb[i], q[i]], axis=0).astype(BF16), k[i].astype(BF16)) for i in n]
        lp = [jnp.where(strict_g, m[i][0:CG] * dec[i], 0.0) for i in n]
        for i, (j, h) in enumerate(probs):
            aqs[rows[j], h * D_HEAD:h * D_HEAD + CG] = jnp.where(incl_g, m[i][CG:2 * CG] * dec[i], 0.0)
        p = [-lp[i] for i in n]
        lpb = [lp[i].astype(BF16) for i in n]
        for step in range(n_neumann):
            fill()
            lp = [_dot(lpb[i], lpb[i]) for i in n]
            lpb = [lp[i].astype(BF16) for i in n]
            p = [p[i] + lp[i] + _dot(p[i].astype(BF16), lpb[i]) for i in n]
        rhs = [jnp.concatenate([v[i] * b_col[i], kb[i] * eg_col[i]], axis=1) for i in n]
        sol = [rhs[i] + _dot(p[i].astype(BF16), rhs[i].astype(BF16)) for i in n]
        for i, (j, h) in enumerate(probs):
            qkv[V_SLAB + h, rows[j], :] = sol[i][:, 0:D_HEAD]
            kcs[rows[j], _hs(0, h)] = sol[i][:, D_HEAD:2 * D_HEAD]
            qkv[Q_SLAB + h, rows[j], :] = q[i] * eg_col[i]
            qkv[K_SLAB + h, rows[j], :] = k[i] * e_to_end[j][:, h:h + 1]

    assert (R // CG) % NB == 0
    for it in range((R // CG) // NB):
        gdn_prepare(it)

    rec = [(bb, h) for bb in seqs for h in heads]

    def gdn_recur(c):
        need("gate_a")
        n = range(len(rec))
        rows = [slice(bb * T + c * CG, bb * T + (c + 1) * CG) for bb, h in rec]
        e_rows = [els[(bb * (T // CG) + c) * 8:(bb * (T // CG) + c) * 8 + 1, :] for bb in seqs]
        e_last = [e_rows[bb][:, h:h + 1] for bb, h in rec]
        s = [sg_ref[bb, h] for bb, h in rec]
        lhs = [jnp.concatenate([kcs[rows[i], _hs(0, h)], qkv[Q_SLAB + h, rows[i], :]], axis=0).astype(BF16)
               for i, (bb, h) in enumerate(rec)]
        qs = [_dot(lhs[i], s[i].astype(BF16)) for i in n]
        fill()
        v_new = [(qkv[V_SLAB + h, rows[i], :] - qs[i][0:CG]).astype(BF16) for i, (bb, h) in enumerate(rec)]
        o = [qs[i][CG:2 * CG] + _dot(aqs[rows[i], h * D_HEAD:h * D_HEAD + CG].astype(BF16), v_new[i])
             for i, (bb, h) in enumerate(rec)]
        upd = [_dot_tn(qkv[K_SLAB + h, rows[i], :].astype(BF16), v_new[i]) for i, (bb, h) in enumerate(rec)]
        fill()
        for i, (bb, h) in enumerate(rec):
            sg_ref[bb, h] = s[i] * e_last[i] + upd[i]
            ms = jnp.mean(o[i] * o[i], axis=-1, keepdims=True)
            gate = pr[rows[i], _hs(R_GA, h)]
            oa_ref[bb, c * CG:(c + 1) * CG, _hs(0, h)] = (
                o[i] * lax.rsqrt(ms + RMS_EPS) * gnw * (gate * _sigmoid(gate))).astype(BF16)

    n_rec = T // CG
    for p in range(N_HEADS // 2):
        hgrn_pair(p)
        if p < n_rec:
            gdn_recur(p)
    for c in range(N_HEADS // 2, n_rec):
        gdn_recur(c)
    fill(len(pending))

    @pl.when(i == nt - 1)
    def _hgrn_state():
        for bb in seqs:
            for h in heads:
                sh_ref[bb, h] = sht[bb, h].T


def _ffn_kernel(FC, NP, x_ref, oa_ref, ob_ref, mg_ref, wa_ref, wb_ref, wo_ref, g1_ref, b1_ref,
                wgu_ref, wd_ref, g2_ref, b2_ref, y_ref):
    rows_per = x_ref.shape[0] // NP
    parts = range(NP)
    rows = [slice(p * rows_per, (p + 1) * rows_per) for p in parts]
    ya = [_dot(oa_ref[rows[p], :], wa_ref[...]) for p in parts]
    yb = [_dot(ob_ref[rows[p], :], wb_ref[...]) for p in parts]
    merged = [(_sigmoid(mg_ref[rows[p], 0:D_MODEL].astype(F32)) * ya[p]
               + _sigmoid(mg_ref[rows[p], D_MODEL:2 * D_MODEL].astype(F32)) * yb[p]).astype(BF16) for p in parts]
    mix = [_dot(merged[p], wo_ref[...]) for p in parts]
    x1 = [_layer_norm(ALPHA * x_ref[rows[p], :] + mix[p], g1_ref[...], b1_ref[...]) for p in parts]
    x1b = [x1[p].astype(BF16) for p in parts]
    acc = [jnp.zeros(x1[p].shape, F32) for p in parts]
    for j in range(D_FF // FC):
        gate = [_dot(x1b[p], wgu_ref[:, j * FC:(j + 1) * FC]) for p in parts]
        up = [_dot(x1b[p], wgu_ref[:, D_FF + j * FC:D_FF + (j + 1) * FC]) for p in parts]
        act = [(gate[p] * _sigmoid(gate[p]) * up[p]).astype(BF16) for p in parts]
        acc = [acc[p] + _dot(act[p], wd_ref[j * FC:(j + 1) * FC, :]) for p in parts]
    for p in parts:
        y_ref[rows[p], :] = _layer_norm(ALPHA * x1[p] + acc[p], g2_ref[...], b2_ref[...])


def _const_spec(shape):
    nd = len(shape)
    return pl.BlockSpec(shape, lambda *_: (0,) * nd, pipeline_mode=pl.Buffered(1))


def _tiles(B, L):
    T = min(128, L)
    if L == T and B * T <= 128:
        BB = B
    else:
        BB = 2 if B % 2 == 0 else 1
    assert L % T == 0 and L >= CONV_W - 1 and T % 8 == 0
    CG = min(64, T)
    CH = min(16, T)
    assert T % CG == 0 and T % CH == 0 and CG & (CG - 1) == 0 and CH & (CH - 1) == 0 and CH >= 8
    return BB, T, CG, CH


def _layer(layer, x, conv_buf, s_gdn, s_hgrn, w_in_p, conv_w, a_log, dt_bias, gdn_norm_w, lb_logits,
           hgrn_norm_w, w_br_a, w_br_b, w_out, ln1_g, ln1_b, w_gate_up, w_down, ln2_g, ln2_b):
    B, L, _ = x.shape
    BB, T, CG, CH = _tiles(B, L)
    nt = L // T
    R = BB * T
    cache8 = jnp.pad(conv_buf.astype(F32), ((0, 0), (8 - (CONV_W - 1), 0), (0, 0)))
    row128 = lambda v: jnp.pad(v.astype(F32).reshape(1, -1), ((0, 0), (0, 128 - v.shape[-1])))

    seq = lambda w: pl.BlockSpec((BB, T, w), lambda b, i: (b, i, 0))
    per_b = lambda *s: pl.BlockSpec((BB,) + s, lambda b, i: (b,) + (0,) * len(s))
    oa, ob, mg, conv_new, sg_new, sh_new = pl.pallas_call(
        functools.partial(_mixer_kernel, layer, BB, T, CG, CH, nt),
        grid=(B // BB, nt),
        in_specs=[seq(D_MODEL), per_b(8, CONV_CH), per_b(N_HEADS, D_HEAD, D_HEAD), per_b(N_HEADS, D_HEAD, D_HEAD),
                  _const_spec((D_MODEL, P_TOTAL)), _const_spec((CONV_W, CONV_CH)), _const_spec((1, 128)),
                  _const_spec((1, 128)), _const_spec((1, 128)), _const_spec(lb_logits.shape),
                  _const_spec((1, 128))],
        out_specs=[seq(D_MODEL), seq(D_MODEL), seq(2 * D_MODEL), per_b(CONV_W - 1, CONV_CH),
                   per_b(N_HEADS, D_HEAD, D_HEAD), per_b(N_HEADS, D_HEAD, D_HEAD)],
        out_shape=[jax.ShapeDtypeStruct((B, L, D_MODEL), BF16), jax.ShapeDtypeStruct((B, L, D_MODEL), BF16),
                   jax.ShapeDtypeStruct((B, L, 2 * D_MODEL), BF16),
                   jax.ShapeDtypeStruct((B, CONV_W - 1, CONV_CH), F32),
                   jax.ShapeDtypeStruct((B, N_HEADS, D_HEAD, D_HEAD), F32),
                   jax.ShapeDtypeStruct((B, N_HEADS, D_HEAD, D_HEAD), F32)],
        scratch_shapes=[pltpu.VMEM((R, D_MODEL), BF16),
                        pltpu.VMEM((BB, N_SLABS, T + 8, 128), F32),
                        pltpu.VMEM((N_SLABS, R, 128), F32),
                        pltpu.VMEM((R, R_TOTAL), F32),
                        pltpu.VMEM((R, 4 * D_MODEL), F32),
                        pltpu.VMEM((R, D_MODEL), F32),
                        pltpu.VMEM((R, D_MODEL), F32),
                        pltpu.VMEM((8 * (R // CG), 128), F32),
                        pltpu.VMEM((BB, N_HEADS, D_HEAD, D_HEAD), F32)],
        compiler_params=pltpu.CompilerParams(dimension_semantics=("parallel", "arbitrary"),
                                             vmem_limit_bytes=VMEM_LIMIT_BYTES),
        name="mixer",
    )(x, cache8, s_gdn.astype(F32), s_hgrn.astype(F32), w_in_p, conv_w.astype(F32), row128(a_log),
      row128(dt_bias), gdn_norm_w.astype(F32).reshape(1, -1), lb_logits.astype(F32),
      hgrn_norm_w.astype(F32).reshape(1, -1))

    N = B * L
    T2 = min(512, N)
    assert N % T2 == 0
    FC = 1408
    NP = 2 if T2 % 512 == 0 else 1
    tok = lambda w: pl.BlockSpec((T2, w), lambda i: (i, 0))
    row = lambda v: v.astype(F32).reshape(1, -1)
    y = pl.pallas_call(
        functools.partial(_ffn_kernel, FC, NP),
        grid=(N // T2,),
        in_specs=[tok(D_MODEL), tok(D_MODEL), tok(D_MODEL), tok(2 * D_MODEL),
                  _const_spec((D_MODEL, D_MODEL)), _const_spec((D_MODEL, D_MODEL)), _const_spec((D_MODEL, D_MODEL)),
                  _const_spec((1, D_MODEL)), _const_spec((1, D_MODEL)),
                  _const_spec((D_MODEL, 2 * D_FF)), _const_spec((D_FF, D_MODEL)),
                  _const_spec((1, D_MODEL)), _const_spec((1, D_MODEL))],
        out_specs=tok(D_MODEL),
        out_shape=jax.ShapeDtypeStruct((N, D_MODEL), F32),
        compiler_params=pltpu.CompilerParams(dimension_semantics=("parallel",),
                                             vmem_limit_bytes=VMEM_LIMIT_BYTES),
        name="ffn",
    )(x.reshape(N, D_MODEL), oa.reshape(N, D_MODEL), ob.reshape(N, D_MODEL), mg.reshape(N, 2 * D_MODEL),
      w_br_a, w_br_b, w_out, row(ln1_g), row(ln1_b), w_gate_up, w_down, row(ln2_g), row(ln2_b))
    return y.reshape(B, L, D_MODEL), conv_new, sg_new, sh_new


def _permute_w_in(w):
    n_ab = 2 * N_HEADS
    w = w.astype(BF16)
    head = w[:, 0:4 * D_MODEL]
    ab = w[:, 4 * D_MODEL:4 * D_MODEL + n_ab]
    tail = w[:, 4 * D_MODEL + n_ab:]
    pad = jnp.zeros((w.shape[0], 128 - n_ab), w.dtype)
    return jnp.concatenate([head, tail, ab, pad], axis=1)


def kernel(x_prompt, x_sample, cache_gdn_conv, state_gdn, state_hgrn, w_in, conv_w, a_log, dt_bias, gdn_norm_w, hgrn_lb_logits, hgrn_norm_w, w_br_a, w_br_b, w_out, ln1_g, ln1_b, w_gate_up, w_down, ln2_g, ln2_b):
    depth = w_in.shape[0]
    assert depth == 1, "ALPHA is baked for a single layer"
    B = x_prompt.shape[0]
    dt = x_prompt.dtype
    y_p, y_s = x_prompt, x_sample
    outs = [[] for _ in range(6)]
    for l in range(depth):
        wl = (_permute_w_in(w_in[l]), conv_w[l], a_log[l], dt_bias[l], gdn_norm_w[l], hgrn_lb_logits,
              hgrn_norm_w[l], w_br_a[l].astype(BF16), w_br_b[l].astype(BF16), w_out[l].astype(BF16),
              ln1_g[l], ln1_b[l], w_gate_up[l].astype(BF16), w_down[l].astype(BF16), ln2_g[l], ln2_b[l])
        zc = jnp.zeros((B, CONV_W - 1, CONV_CH), F32)
        zs = jnp.zeros((B, N_HEADS, D_HEAD, D_HEAD), F32)
        y_p, cp, gp, hp = _layer(l, y_p, zc, zs, zs, *wl)
        y_s, cs, gs, hs = _layer(l, y_s, cache_gdn_conv[l], state_gdn[l], state_hgrn[l], *wl)
        for lst, v in zip(outs, (cp, gp, hp, cs, gs, hs)):
            lst.append(v.astype(dt))
    return (y_p.astype(dt), y_s.astype(dt)) + tuple(jnp.stack(o) for o in outs)
```

```python
import functools
import math

import jax
import jax.numpy as jnp
from jax import lax
from jax.experimental import pallas as pl
from jax.experimental.pallas import tpu as pltpu

F32 = jnp.float32
BF16 = jnp.bfloat16

D_MODEL = 1024
N_HEADS = 8
D_HEAD = 128
CONV_W = 4
CONV_CH = 3 * D_MODEL
D_FF = 2816
ALPHA = 2.0 ** 0.25
LN_EPS = 1e-5
RMS_EPS = 1e-6
L2_EPS = 1e-6
EXP_CLAMP = 80.0
GDN_BATCH = 4

P_QKV = 0
P_REST = CONV_CH
P_MG = P_REST + 5 * D_MODEL
P_AB = P_MG + 2 * D_MODEL
P_TOTAL = P_AB + 128
P_QH, P_FH, P_IH, P_GH = (P_REST + s * D_MODEL for s in range(1, 5))
R_GA, R_AB = 0, 1024
R_TOTAL = R_AB + 128
R_QH, R_FH, R_IH, R_GH = 0, 1024, 2048, 3072
PIECE = 256
N_SLABS = CONV_CH // 128
Q_SLAB, K_SLAB, V_SLAB = 0, N_HEADS, 2 * N_HEADS

VMEM_LIMIT_BYTES = 60 * 1024 * 1024


def _dot(a, b):
    return jnp.dot(a, b, preferred_element_type=F32)


def _dot_nt(a, b):
    return lax.dot_general(a, b, (((1,), (1,)), ((), ())), preferred_element_type=F32)


def _dot_tn(a, b):
    return lax.dot_general(a, b, (((0,), (0,)), ((), ())), preferred_element_type=F32)


def _sigmoid(x):
    return 1.0 / (1.0 + jnp.exp(-x))


def _split_bf16(x):
    hi = x.astype(BF16)
    lo = (x - hi.astype(F32)).astype(BF16)
    return hi, lo


def _layer_norm(x, g, b):
    mu = jnp.mean(x, axis=-1, keepdims=True)
    xc = x - mu
    var = jnp.mean(xc * xc, axis=-1, keepdims=True)
    return xc * lax.rsqrt(var + LN_EPS) * g + b


def _hs(off, h):
    return slice(off + h * D_HEAD, off + (h + 1) * D_HEAD)


def _mixer_kernel(layer, BB, T, CG, CH, nt,
                  x_ref, cache_ref, sg0_ref, sh0_ref, w_ref, cw_ref, alog_ref, dtb_ref, gnw_ref,
                  lbl_ref, hnw_ref,
                  oa_ref, ob_ref, mg_ref, conv_ref, sg_ref, sh_ref,
                  xbs, cb, qkv, pr, hp, kcs, aqs, els, sht):
    i = pl.program_id(1)
    heads = range(N_HEADS)
    seqs = range(BB)
    R = BB * T

    @pl.when(i == 0)
    def _init():
        for bb in seqs:
            for blk in range(N_SLABS):
                cb[bb, blk, 0:8, :] = cache_ref[bb, :, blk * 128:(blk + 1) * 128]
            for h in heads:
                sht[bb, h] = sh0_ref[bb, h].T
        sg_ref[...] = sg0_ref[...]

    xbs[...] = x_ref[...].reshape(R, D_MODEL).astype(BF16)

    def project(c0, width):
        return _dot(xbs[...], w_ref[:, c0:c0 + width])

    pending = []

    def add_pieces(tag, c0, width, store):
        for c in range(0, width, PIECE):
            pending.append((tag, lambda c=c: store(c, project(c0 + c, PIECE))))

    def fill(n=1):
        for _ in range(min(n, len(pending))):
            pending.pop(0)[1]()

    def need(tag):
        while any(t == tag for t, _ in pending):
            pending.pop(0)[1]()

    def store_hp(off):
        def store(c, val):
            hp[:, off + c:off + c + PIECE] = val
        return store

    def store_ga(c, val):
        pr[:, R_GA + c:R_GA + c + PIECE] = val

    def store_mg(c, val):
        v16 = val.astype(BF16)
        for bb in seqs:
            mg_ref[bb, :, c:c + PIECE] = v16[bb * T:(bb + 1) * T]

    def add_pair(p):
        for s, c0 in enumerate((P_QH, P_FH, P_IH, P_GH)):
            add_pieces(("pair", p), c0 + p * 2 * D_HEAD, 2 * D_HEAD, store_hp(s * D_MODEL + p * 2 * D_HEAD))

    add_pair(0)
    add_pair(1)
    add_pieces("gate_a", P_REST, D_MODEL, store_ga)
    add_pair(2)
    add_pair(3)
    add_pieces("merge", P_MG, 2 * D_MODEL, store_mg)

    pr[:, R_AB:R_AB + 128] = project(P_AB, 128)
    for s in range(3):
        u = project(P_QKV + s * 1024, 1024)
        for bb in seqs:
            for j in heads:
                cb[bb, s * N_HEADS + j, 8:8 + T, :] = u[bb * T:(bb + 1) * T, j * 128:(j + 1) * 128]

    for blk in range(N_SLABS):
        cs = slice(blk * 128, (blk + 1) * 128)
        wj = [cw_ref[j:j + 1, cs] for j in range(CONV_W)]
        for bb in seqs:
            taps = {m: cb[bb, blk, pl.ds(m, T // 8, stride=8), :] for m in range(5, 16)}
            for r in range(8):
                y = taps[r + 5] * wj[0] + taps[r + 6] * wj[1] + taps[r + 7] * wj[2] + taps[r + 8] * wj[3]
                y = y * _sigmoid(y)
                if blk < 2 * N_HEADS:
                    y = y * lax.rsqrt(jnp.sum(y * y, axis=-1, keepdims=True) + L2_EPS)
                    if blk < N_HEADS:
                        y = y * (D_HEAD ** -0.5)
                qkv[blk, pl.ds(bb * T + r, T // 8, stride=8), :] = y
        if blk >= N_HEADS and blk % 4 == 3:
            fill()

    @pl.when(i == nt - 1)
    def _conv_state():
        for bb in seqs:
            for blk in range(N_SLABS):
                conv_ref[bb, :, blk * 128:(blk + 1) * 128] = cb[bb, blk, T + 5:T + 8, :]

    cb[:, :, 0:8, :] = cb[:, :, T:T + 8, :]

    ll = lbl_ref[...]
    el = jnp.exp(ll - jnp.max(ll, axis=0, keepdims=True))
    lb = jnp.sum(el[0:layer + 1], axis=0, keepdims=True) / jnp.sum(el, axis=0, keepdims=True)
    hnw = hnw_ref[...]
    row_t = lax.broadcasted_iota(jnp.int32, (T, T), 0)
    col_t = lax.broadcasted_iota(jnp.int32, (T, T), 1)
    xor_t = row_t ^ col_t
    before_t = col_t < row_t
    causal_t = row_t >= col_t
    near_t = causal_t & (xor_t < CH)
    tril_t = jnp.concatenate([causal_t.astype(BF16)] * 2, axis=1)
    widths = [w for w in (2 ** e for e in range(1, 16)) if CH < w <= T]
    level_mask = [before_t & (xor_t >= w // 2) & (xor_t < w) for w in widths]

    def block_row(x, w, r):
        x3 = x.reshape(T // w, w, D_HEAD)
        return jnp.broadcast_to(x3[:, r:r + 1, :], x3.shape).reshape(T, D_HEAD)

    def hgrn_pair(p):
        need(("pair", p))
        probs_h = [(h, bb) for h in (2 * p, 2 * p + 1) for bb in seqs]
        n = range(len(probs_h))
        rows = [slice(bb * T, (bb + 1) * T) for h, bb in probs_h]
        lbh = [lb[:, _hs(0, h)] for h, bb in probs_h]
        z = [hp[rows[i], _hs(R_FH, h)] for i, (h, bb) in enumerate(probs_h)]
        e = [jnp.exp(-jnp.abs(z[i])) for i in n]
        r = [1.0 / (1.0 + e[i]) for i in n]
        sig_pos = [jnp.where(z[i] >= 0.0, r[i], e[i] * r[i]) for i in n]
        sig_neg = [jnp.where(z[i] >= 0.0, e[i] * r[i], r[i]) for i in n]
        logf = [jnp.log(lbh[i] + (1.0 - lbh[i]) * sig_pos[i]) for i in n]
        k = [(1.0 - lbh[i]) * sig_neg[i] for i in n]
        qh = [hp[rows[i], _hs(R_QH, h)] for i, (h, bb) in enumerate(probs_h)]
        q = [qh[i] * _sigmoid(qh[i]) * (D_HEAD ** -0.5) for i in n]
        vb = [hp[rows[i], _hs(R_IH, h)].astype(BF16) for i, (h, bb) in enumerate(probs_h)]
        split = [_split_bf16(logf[i]) for i in n]
        b = [_dot(tril_t, jnp.concatenate(split[i], axis=0)) for i in n]
        fill(2)
        bm = [b[i] - block_row(b[i], CH, CH // 2 - 1) for i in n]
        q_in = [(q[i] * jnp.exp(jnp.minimum(bm[i], EXP_CLAMP))).astype(BF16) for i in n]
        k_in = [(k[i] * jnp.exp(jnp.minimum(-bm[i], EXP_CLAMP))).astype(BF16) for i in n]
        a = [jnp.where(near_t, _dot_nt(q_in[i], k_in[i]), 0.0) for i in n]
        for w, mask in zip(widths, level_mask):
            hw = w // 2
            q_w, k_w = [], []
            for i in n:
                b3, q3, k3 = (x.reshape(T // w, w, D_HEAD) for x in (b[i], q[i], k[i]))
                ref = b3[:, hw - 1:hw, :]
                q_up = (q3[:, hw:, :] * jnp.exp(b3[:, hw:, :] - ref)).astype(BF16)
                k_lo = (k3[:, :hw, :] * jnp.exp(ref - b3[:, :hw, :])).astype(BF16)
                zero = jnp.zeros((T // w, hw, D_HEAD), BF16)
                q_w.append(jnp.concatenate([zero, q_up], axis=1).reshape(T, D_HEAD))
                k_w.append(jnp.concatenate([k_lo, zero], axis=1).reshape(T, D_HEAD))
            a = [jnp.where(mask, _dot_nt(q_w[i], k_w[i]), a[i]) for i in n]
        fill(2)
        b_end = [b[i][T - 1:T, :] for i in n]
        st = [sht[bb, h] for h, bb in probs_h]
        q_st = [(q[i] * jnp.exp(b[i])).astype(BF16) for i in n]
        k_st = [(k[i] * jnp.exp(b_end[i] - b[i])).astype(BF16) for i in n]
        o = [_dot(a[i].astype(BF16), vb[i]) + _dot_nt(q_st[i], st[i].astype(BF16)) for i in n]
        upd = [_dot_tn(vb[i], k_st[i]) for i in n]
        for i, (h, bb) in enumerate(probs_h):
            sht[bb, h] = st[i] * jnp.exp(b_end[i]) + upd[i]
            ms = jnp.mean(o[i] * o[i], axis=-1, keepdims=True)
            gate = hp[rows[i], _hs(R_GH, h)]
            ob_ref[bb, :, _hs(0, h)] = (o[i] * lax.rsqrt(ms + RMS_EPS) * hnw * _sigmoid(gate)).astype(BF16)

    neg_a = -jnp.exp(alog_ref[...])
    dtb = dtb_ref[...]
    gnw = gnw_ref[...]
    lane_g = lax.broadcasted_iota(jnp.int32, (CG, 128), 1)
    row_g = lax.broadcasted_iota(jnp.int32, (CG, CG), 0)
    col_g = lax.broadcasted_iota(jnp.int32, (CG, CG), 1)
    incl_g = row_g >= col_g
    strict_g = row_g > col_g
    tril_g = jnp.concatenate([incl_g.astype(BF16)] * 2, axis=1)
    n_neumann = int(math.log2(CG)) - 1

    NB = min(GDN_BATCH, R // CG)
    probs = [(j, h) for j in range(NB) for h in heads]

    def gdn_prepare(it):
        rows, gc, gc_t, egc, e_to_end, beta = [], [], [], [], [], []
        for j in range(NB):
            c = it * NB + j
            rows.append(slice(c * CG, (c + 1) * CG))
            ab = pr[rows[j], R_AB:R_AB + 128]
            xa = ab + dtb
            softplus = jnp.maximum(xa, 0.0) + jnp.log(1.0 + jnp.exp(-jnp.abs(xa)))
            g = jnp.where(lane_g < N_HEADS, neg_a * softplus, 0.0)
            beta.append(_sigmoid(ab))
            gcj = _dot(tril_g, jnp.concatenate(_split_bf16(g), axis=0))
            gc.append(gcj)
            gc_sq = jnp.concatenate([gcj, jnp.zeros((128 - CG, 128), F32)], axis=0) if CG < 128 else gcj
            gc_t.append(gc_sq.T)
            egc.append(jnp.exp(gcj))
            g_last = gcj[CG - 1:CG, :]
            e_to_end.append(jnp.exp(g_last - gcj))
            els[c * 8:(c + 1) * 8, :] = jnp.broadcast_to(jnp.exp(g_last), (8, 128))

        q = [qkv[Q_SLAB + h, rows[j], :] for j, h in probs]
        k = [qkv[K_SLAB + h, rows[j], :] for j, h in probs]
        v = [qkv[V_SLAB + h, rows[j], :] for j, h in probs]
        b_col = [beta[j][:, N_HEADS + h:N_HEADS + h + 1] for j, h in probs]
        eg_col = [egc[j][:, h:h + 1] for j, h in probs]
        dec = [jnp.exp(jnp.minimum(gc[j][:, h:h + 1] - gc_t[j][h:h + 1, 0:CG], 0.0)) for j, h in probs]
        n = range(len(probs))
        kb = [k[i] * b_col[i] for i in n]
        m = [_dot_nt(jnp.concatenate([kb[i], q[i]], axis=0).astype(BF16), k[i].astype(BF16)) for i in n]
        lp = [jnp.where(strict_g, m[i][0:CG] * dec[i], 0.0) for i in n]
        for i, (j, h) in enumerate(probs):
            aqs[rows[j], h * D_HEAD:h * D_HEAD + CG] = jnp.where(incl_g, m[i][CG:2 * CG] * dec[i], 0.0)
        p = [-lp[i] for i in n]
        lpb = [lp[i].astype(BF16) for i in n]
        for step in range(n_neumann):
            fill()
            lp = [_dot(lpb[i], lpb[i]) for i in n]
            lpb = [lp[i].astype(BF16) for i in n]
            p = [p[i] + lp[i] + _dot(p[i].astype(BF16), lpb[i]) for i in n]
        rhs = [jnp.concatenate([v[i] * b_col[i], kb[i] * eg_col[i]], axis=1) for i in n]
        sol = [rhs[i] + _dot(p[i].astype(BF16), rhs[i].astype(BF16)) for i in n]
        for i, (j, h) in enumerate(probs):
            qkv[V_SLAB + h, rows[j], :] = sol[i][:, 0:D_HEAD]
            kcs[rows[j], _hs(0, h)] = sol[i][:, D_HEAD:2 * D_HEAD]
            qkv[Q_SLAB + h, rows[j], :] = q[i] * eg_col[i]
            qkv[K_SLAB + h, rows[j], :] = k[i] * e_to_end[j][:, h:h + 1]

    assert (R // CG) % NB == 0
    for it in range((R // CG) // NB):
        gdn_prepare(it)

    rec = [(bb, h) for bb in seqs for h in heads]

    def gdn_recur(c):
        need("gate_a")
        n = range(len(rec))
        rows = [slice(bb * T + c * CG, bb * T + (c + 1) * CG) for bb, h in rec]
        e_rows = [els[(bb * (T // CG) + c) * 8:(bb * (T // CG) + c) * 8 + 1, :] for bb in seqs]
        e_last = [e_rows[bb][:, h:h + 1] for bb, h in rec]
        s = [sg_ref[bb, h] for bb, h in rec]
        lhs = [jnp.concatenate([kcs[rows[i], _hs(0, h)], qkv[Q_SLAB + h, rows[i], :]], axis=0).astype(BF16)
               for i, (bb, h) in enumerate(rec)]
        qs = [_dot(lhs[i], s[i].astype(BF16)) for i in n]
        fill()
        v_new = [(qkv[V_SLAB + h, rows[i], :] - qs[i][0:CG]).astype(BF16) for i, (bb, h) in enumerate(rec)]
        o = [qs[i][CG:2 * CG] + _dot(aqs[rows[i], h * D_HEAD:h * D_HEAD + CG].astype(BF16), v_new[i])
             for i, (bb, h) in enumerate(rec)]
        upd = [_dot_tn(qkv[K_SLAB + h, rows[i], :].astype(BF16), v_new[i]) for i, (bb, h) in enumerate(rec)]
        fill()
        for i, (bb, h) in enumerate(rec):
            sg_ref[bb, h] = s[i] * e_last[i] + upd[i]
            ms = jnp.mean(o[i] * o[i], axis=-1, keepdims=True)
            gate = pr[rows[i], _hs(R_GA, h)]
            oa_ref[bb, c * CG:(c + 1) * CG, _hs(0, h)] = (
                o[i] * lax.rsqrt(ms + RMS_EPS) * gnw * (gate * _sigmoid(gate))).astype(BF16)

    n_rec = T // CG
    for p in range(N_HEADS // 2):
        hgrn_pair(p)
        if p < n_rec:
            gdn_recur(p)
    for c in range(N_HEADS // 2, n_rec):
        gdn_recur(c)
    fill(len(pending))

    @pl.when(i == nt - 1)
    def _hgrn_state():
        for bb in seqs:
            for h in heads:
                sh_ref[bb, h] = sht[bb, h].T


def _ffn_kernel(FC, NP, x_ref, oa_ref, ob_ref, mg_ref, wa_ref, wb_ref, wo_ref, g1_ref, b1_ref,
                wgu_ref, wd_ref, g2_ref, b2_ref, y_ref):
    rows_per = x_ref.shape[0] // NP
    parts = range(NP)
    rows = [slice(p * rows_per, (p + 1) * rows_per) for p in parts]
    ya = [_dot(oa_ref[rows[p], :], wa_ref[...]) for p in parts]
    yb = [_dot(ob_ref[rows[p], :], wb_ref[...]) for p in parts]
    merged = [(_sigmoid(mg_ref[rows[p], 0:D_MODEL].astype(F32)) * ya[p]
               + _sigmoid(mg_ref[rows[p], D_MODEL:2 * D_MODEL].astype(F32)) * yb[p]).astype(BF16) for p in parts]
    mix = [_dot(merged[p], wo_ref[...]) for p in parts]
    x1 = [_layer_norm(ALPHA * x_ref[rows[p], :] + mix[p], g1_ref[...], b1_ref[...]) for p in parts]
    x1b = [x1[p].astype(BF16) for p in parts]
    acc = [jnp.zeros(x1[p].shape, F32) for p in parts]
    for j in range(D_FF // FC):
        gate = [_dot(x1b[p], wgu_ref[:, j * FC:(j + 1) * FC]) for p in parts]
        up = [_dot(x1b[p], wgu_ref[:, D_FF + j * FC:D_FF + (j + 1) * FC]) for p in parts]
        act = [(gate[p] * _sigmoid(gate[p]) * up[p]).astype(BF16) for p in parts]
        acc = [acc[p] + _dot(act[p], wd_ref[j * FC:(j + 1) * FC, :]) for p in parts]
    for p in parts:
        y_ref[rows[p], :] = _layer_norm(ALPHA * x1[p] + acc[p], g2_ref[...], b2_ref[...])


def _const_spec(shape):
    nd = len(shape)
    return pl.BlockSpec(shape, lambda *_: (0,) * nd, pipeline_mode=pl.Buffered(1))


def _tiles(B, L):
    T = min(128, L)
    if L == T and B * T <= 128:
        BB = B
    else:
        BB = 2 if B % 2 == 0 else 1
    assert L % T == 0 and L >= CONV_W - 1 and T % 8 == 0
    CG = min(64, T)
    CH = min(16, T)
    assert T % CG == 0 and T % CH == 0 and CG & (CG - 1) == 0 and CH & (CH - 1) == 0 and CH >= 8
    return BB, T, CG, CH


def _layer(layer, x, conv_buf, s_gdn, s_hgrn, w_in_p, conv_w, a_log, dt_bias, gdn_norm_w, lb_logits,
           hgrn_norm_w, w_br_a, w_br_b, w_out, ln1_g, ln1_b, w_gate_up, w_down, ln2_g, ln2_b):
    B, L, _ = x.shape
    BB, T, CG, CH = _tiles(B, L)
    nt = L // T
    R = BB * T
    cache8 = jnp.pad(conv_buf.astype(F32), ((0, 0), (8 - (CONV_W - 1), 0), (0, 0)))
    row128 = lambda v: jnp.pad(v.astype(F32).reshape(1, -1), ((0, 0), (0, 128 - v.shape[-1])))

    seq = lambda w: pl.BlockSpec((BB, T, w), lambda b, i: (b, i, 0))
    per_b = lambda *s: pl.BlockSpec((BB,) + s, lambda b, i: (b,) + (0,) * len(s))
    oa, ob, mg, conv_new, sg_new, sh_new = pl.pallas_call(
        functools.partial(_mixer_kernel, layer, BB, T, CG, CH, nt),
        grid=(B // BB, nt),
        in_specs=[seq(D_MODEL), per_b(8, CONV_CH), per_b(N_HEADS, D_HEAD, D_HEAD), per_b(N_HEADS, D_HEAD, D_HEAD),
                  _const_spec((D_MODEL, P_TOTAL)), _const_spec((CONV_W, CONV_CH)), _const_spec((1, 128)),
                  _const_spec((1, 128)), _const_spec((1, 128)), _const_spec(lb_logits.shape),
                  _const_spec((1, 128))],
        out_specs=[seq(D_MODEL), seq(D_MODEL), seq(2 * D_MODEL), per_b(CONV_W - 1, CONV_CH),
                   per_b(N_HEADS, D_HEAD, D_HEAD), per_b(N_HEADS, D_HEAD, D_HEAD)],
        out_shape=[jax.ShapeDtypeStruct((B, L, D_MODEL), BF16), jax.ShapeDtypeStruct((B, L, D_MODEL), BF16),
                   jax.ShapeDtypeStruct((B, L, 2 * D_MODEL), BF16),
                   jax.ShapeDtypeStruct((B, CONV_W - 1, CONV_CH), F32),
                   jax.ShapeDtypeStruct((B, N_HEADS, D_HEAD, D_HEAD), F32),
                   jax.ShapeDtypeStruct((B, N_HEADS, D_HEAD, D_HEAD), F32)],
        scratch_shapes=[pltpu.VMEM((R, D_MODEL), BF16),
                        pltpu.VMEM((BB, N_SLABS, T + 8, 128), F32),
                        pltpu.VMEM((N_SLABS, R, 128), F32),
                        pltpu.VMEM((R, R_TOTAL), F32),
                        pltpu.VMEM((R, 4 * D_MODEL), F32),
                        pltpu.VMEM((R, D_MODEL), F32),
                        pltpu.VMEM((R, D_MODEL), F32),
                        pltpu.VMEM((8 * (R // CG), 128), F32),
                        pltpu.VMEM((BB, N_HEADS, D_HEAD, D_HEAD), F32)],
        compiler_params=pltpu.CompilerParams(dimension_semantics=("parallel", "arbitrary"),
                                             vmem_limit_bytes=VMEM_LIMIT_BYTES),
        name="mixer",
    )(x, cache8, s_gdn.astype(F32), s_hgrn.astype(F32), w_in_p, conv_w.astype(F32), row128(a_log),
      row128(dt_bias), gdn_norm_w.astype(F32).reshape(1, -1), lb_logits.astype(F32),
      hgrn_norm_w.astype(F32).reshape(1, -1))

    N = B * L
    T2 = min(512, N)
    assert N % T2 == 0
    FC = 1408
    NP = 2 if T2 % 512 == 0 else 1
    tok = lambda w: pl.BlockSpec((T2, w), lambda i: (i, 0))
    row = lambda v: v.astype(F32).reshape(1, -1)
    y = pl.pallas_call(
        functools.partial(_ffn_kernel, FC, NP),
        grid=(N // T2,),
        in_specs=[tok(D_MODEL), tok(D_MODEL), tok(D_MODEL), tok(2 * D_MODEL),
                  _const_spec((D_MODEL, D_MODEL)), _const_spec((D_MODEL, D_MODEL)), _const_spec((D_MODEL, D_MODEL)),
                  _const_spec((1, D_MODEL)), _const_spec((1, D_MODEL)),
                  _const_spec((D_MODEL, 2 * D_FF)), _const_spec((D_FF, D_MODEL)),
                  _const_spec((1, D_MODEL)), _const_spec((1, D_MODEL))],
        out_specs=tok(D_MODEL),
        out_shape=jax.ShapeDtypeStruct((N, D_MODEL), F32),
        compiler_params=pltpu.CompilerParams(dimension_semantics=("parallel",),
                                             vmem_limit_bytes=VMEM_LIMIT_BYTES),
        name="ffn",
    )(x.reshape(N, D_MODEL), oa.reshape(N, D_MODEL), ob.reshape(N, D_MODEL), mg.reshape(N, 2 * D_MODEL),
      w_br_a, w_br_b, w_out, row(ln1_g), row(ln1_b), w_gate_up, w_down, row(ln2_g), row(ln2_b))
    return y.reshape(B, L, D_MODEL), conv_new, sg_new, sh_new


def _permute_kernel(w_ref, o_ref):
    n_ab = 2 * N_HEADS
    n_head = 4 * D_MODEL
    n_tail = P_AB - n_head
    o_ref[:, 0:n_head] = w_ref[:, 0:n_head].astype(BF16)
    o_ref[:, n_head:P_AB] = w_ref[:, n_head + n_ab:n_head + n_ab + n_tail].astype(BF16)
    o_ref[:, P_AB:P_TOTAL] = jnp.zeros((o_ref.shape[0], P_TOTAL - P_AB), BF16)
    o_ref[:, P_AB:P_AB + n_ab] = w_ref[:, n_head:n_head + n_ab].astype(BF16)


def _permute_w_in(w):
    rows, cols = w.shape
    rb = 128
    assert rows % rb == 0 and cols == P_TOTAL - 128 + 2 * N_HEADS
    return pl.pallas_call(
        _permute_kernel,
        grid=(rows // rb,),
        in_specs=[pl.BlockSpec((rb, cols), lambda i: (i, 0))],
        out_specs=pl.BlockSpec((rb, P_TOTAL), lambda i: (i, 0)),
        out_shape=jax.ShapeDtypeStruct((rows, P_TOTAL), BF16),
        compiler_params=pltpu.CompilerParams(dimension_semantics=("parallel",)),
        name="permute_w_in",
    )(w)


def kernel(x_prompt, x_sample, cache_gdn_conv, state_gdn, state_hgrn, w_in, conv_w, a_log, dt_bias, gdn_norm_w, hgrn_lb_logits, hgrn_norm_w, w_br_a, w_br_b, w_out, ln1_g, ln1_b, w_gate_up, w_down, ln2_g, ln2_b):
    depth = w_in.shape[0]
    assert depth == 1, "ALPHA is baked for a single layer"
    B = x_prompt.shape[0]
    dt = x_prompt.dtype
    y_p, y_s = x_prompt, x_sample
    outs = [[] for _ in range(6)]
    for l in range(depth):
        wl = (_permute_w_in(w_in[l]), conv_w[l], a_log[l], dt_bias[l], gdn_norm_w[l], hgrn_lb_logits,
              hgrn_norm_w[l], w_br_a[l].astype(BF16), w_br_b[l].astype(BF16), w_out[l].astype(BF16),
              ln1_g[l], ln1_b[l], w_gate_up[l].astype(BF16), w_down[l].astype(BF16), ln2_g[l], ln2_b[l])
        zc = jnp.zeros((B, CONV_W - 1, CONV_CH), F32)
        zs = jnp.zeros((B, N_HEADS, D_HEAD, D_HEAD), F32)
        y_p, cp, gp, hp = _layer(l, y_p, zc, zs, zs, *wl)
        y_s, cs, gs, hs = _layer(l, y_s, cache_gdn_conv[l], state_gdn[l], state_hgrn[l], *wl)
        for lst, v in zip(outs, (cp, gp, hp, cs, gs, hs)):
            lst.append(v.astype(dt))
    return (y_p.astype(dt), y_s.astype(dt)) + tuple(jnp.stack(o) for o in outs)
```

```python
import functools
import math

import jax
import jax.numpy as jnp
from jax import lax
from jax.experimental import pallas as pl
from jax.experimental.pallas import tpu as pltpu

F32 = jnp.float32
BF16 = jnp.bfloat16

D_MODEL = 1024
N_HEADS = 8
D_HEAD = 128
CONV_W = 4
CONV_CH = 3 * D_MODEL
D_FF = 2816
ALPHA = 2.0 ** 0.25
LN_EPS = 1e-5
RMS_EPS = 1e-6
L2_EPS = 1e-6
EXP_CLAMP = 80.0
GDN_BATCH = 4

P_QKV = 0
P_REST = CONV_CH
P_MG = P_REST + 5 * D_MODEL
P_AB = P_MG + 2 * D_MODEL
P_TOTAL = P_AB + 128
P_QH, P_FH, P_IH, P_GH = (P_REST + s * D_MODEL for s in range(1, 5))
R_GA, R_AB = 0, 1024
R_TOTAL = R_AB + 128
R_QH, R_FH, R_IH, R_GH = 0, 1024, 2048, 3072
PIECE = 256
N_SLABS = CONV_CH // 128
Q_SLAB, K_SLAB, V_SLAB = 0, N_HEADS, 2 * N_HEADS

VMEM_LIMIT_BYTES = 60 * 1024 * 1024


def _dot(a, b):
    return jnp.dot(a, b, preferred_element_type=F32)


def _dot_nt(a, b):
    return lax.dot_general(a, b, (((1,), (1,)), ((), ())), preferred_element_type=F32)


def _dot_tn(a, b):
    return lax.dot_general(a, b, (((0,), (0,)), ((), ())), preferred_element_type=F32)


def _sigmoid(x):
    return 1.0 / (1.0 + jnp.exp(-x))


def _split_bf16(x):
    hi = x.astype(BF16)
    lo = (x - hi.astype(F32)).astype(BF16)
    return hi, lo


def _layer_norm(x, g, b):
    mu = jnp.mean(x, axis=-1, keepdims=True)
    xc = x - mu
    var = jnp.mean(xc * xc, axis=-1, keepdims=True)
    return xc * lax.rsqrt(var + LN_EPS) * g + b


def _hs(off, h):
    return slice(off + h * D_HEAD, off + (h + 1) * D_HEAD)


def _mixer_kernel(layer, BB, T, CG, CH, nt,
                  x_ref, cache_ref, sg0_ref, sh0_ref, w_ref, cw_ref, alog_ref, dtb_ref, gnw_ref,
                  lbl_ref, hnw_ref,
                  oa_ref, ob_ref, mg_ref, conv_ref, sg_ref, sh_ref,
                  xbs, cb, qkv, pr, hp, kcs, aqs, els, sht):
    i = pl.program_id(1)
    heads = range(N_HEADS)
    seqs = range(BB)
    R = BB * T

    @pl.when(i == 0)
    def _init():
        for bb in seqs:
            for blk in range(N_SLABS):
                cb[bb, blk, 0:8, :] = cache_ref[bb, :, blk * 128:(blk + 1) * 128]
            for h in heads:
                sht[bb, h] = sh0_ref[bb, h].T
        sg_ref[...] = sg0_ref[...]

    xbs[...] = x_ref[...].reshape(R, D_MODEL).astype(BF16)

    def project(c0, width):
        return _dot(xbs[...], w_ref[:, c0:c0 + width])

    pending = []

    def add_pieces(tag, c0, width, store):
        for c in range(0, width, PIECE):
            pending.append((tag, lambda c=c: store(c, project(c0 + c, PIECE))))

    def fill(n=1):
        for _ in range(min(n, len(pending))):
            pending.pop(0)[1]()

    def need(tag):
        while any(t == tag for t, _ in pending):
            pending.pop(0)[1]()

    def store_hp(off):
        def store(c, val):
            hp[:, off + c:off + c + PIECE] = val
        return store

    def store_ga(c, val):
        pr[:, R_GA + c:R_GA + c + PIECE] = val

    def store_mg(c, val):
        v16 = val.astype(BF16)
        for bb in seqs:
            mg_ref[bb, :, c:c + PIECE] = v16[bb * T:(bb + 1) * T]

    def add_pair(p):
        for s, c0 in enumerate((P_QH, P_FH, P_IH, P_GH)):
            add_pieces(("pair", p), c0 + p * 2 * D_HEAD, 2 * D_HEAD, store_hp(s * D_MODEL + p * 2 * D_HEAD))

    add_pair(0)
    add_pair(1)
    add_pieces("gate_a", P_REST, D_MODEL, store_ga)
    add_pair(2)
    add_pair(3)
    add_pieces("merge", P_MG, 2 * D_MODEL, store_mg)

    pr[:, R_AB:R_AB + 128] = project(P_AB, 128)
    for s in range(3):
        u = project(P_QKV + s * 1024, 1024)
        for bb in seqs:
            for j in heads:
                cb[bb, s * N_HEADS + j, 8:8 + T, :] = u[bb * T:(bb + 1) * T, j * 128:(j + 1) * 128]

    for blk in range(N_SLABS):
        cs = slice(blk * 128, (blk + 1) * 128)
        wj = [cw_ref[j:j + 1, cs] for j in range(CONV_W)]
        for bb in seqs:
            taps = {m: cb[bb, blk, pl.ds(m, T // 8, stride=8), :] for m in range(5, 16)}
            for r in range(8):
                y = taps[r + 5] * wj[0] + taps[r + 6] * wj[1] + taps[r + 7] * wj[2] + taps[r + 8] * wj[3]
                y = y * _sigmoid(y)
                if blk < 2 * N_HEADS:
                    y = y * lax.rsqrt(jnp.sum(y * y, axis=-1, keepdims=True) + L2_EPS)
                    if blk < N_HEADS:
                        y = y * (D_HEAD ** -0.5)
                qkv[blk, pl.ds(bb * T + r, T // 8, stride=8), :] = y
        if blk >= N_HEADS and blk % 4 == 3:
            fill()

    @pl.when(i == nt - 1)
    def _conv_state():
        for bb in seqs:
            for blk in range(N_SLABS):
                conv_ref[bb, :, blk * 128:(blk + 1) * 128] = cb[bb, blk, T + 5:T + 8, :]

    cb[:, :, 0:8, :] = cb[:, :, T:T + 8, :]

    ll = lbl_ref[...]
    el = jnp.exp(ll - jnp.max(ll, axis=0, keepdims=True))
    lb = jnp.sum(el[0:layer + 1], axis=0, keepdims=True) / jnp.sum(el, axis=0, keepdims=True)
    hnw = hnw_ref[...]
    row_t = lax.broadcasted_iota(jnp.int32, (T, T), 0)
    col_t = lax.broadcasted_iota(jnp.int32, (T, T), 1)
    xor_t = row_t ^ col_t
    before_t = col_t < row_t
    causal_t = row_t >= col_t
    near_t = causal_t & (xor_t < CH)
    tril_t = jnp.concatenate([causal_t.astype(BF16)] * 2, axis=1)
    widths = [w for w in (2 ** e for e in range(1, 16)) if CH < w <= T]
    level_mask = [before_t & (xor_t >= w // 2) & (xor_t < w) for w in widths]

    def block_row(x, w, r):
        x3 = x.reshape(T // w, w, D_HEAD)
        return jnp.broadcast_to(x3[:, r:r + 1, :], x3.shape).reshape(T, D_HEAD)

    def hgrn_pair(p):
        need(("pair", p))
        probs_h = [(h, bb) for h in (2 * p, 2 * p + 1) for bb in seqs]
        n = range(len(probs_h))
        rows = [slice(bb * T, (bb + 1) * T) for h, bb in probs_h]
        lbh = [lb[:, _hs(0, h)] for h, bb in probs_h]
        z = [hp[rows[i], _hs(R_FH, h)] for i, (h, bb) in enumerate(probs_h)]
        e = [jnp.exp(-jnp.abs(z[i])) for i in n]
        r = [1.0 / (1.0 + e[i]) for i in n]
        sig_pos = [jnp.where(z[i] >= 0.0, r[i], e[i] * r[i]) for i in n]
        sig_neg = [jnp.where(z[i] >= 0.0, e[i] * r[i], r[i]) for i in n]
        logf = [jnp.log(lbh[i] + (1.0 - lbh[i]) * sig_pos[i]) for i in n]
        k = [(1.0 - lbh[i]) * sig_neg[i] for i in n]
        qh = [hp[rows[i], _hs(R_QH, h)] for i, (h, bb) in enumerate(probs_h)]
        q = [qh[i] * _sigmoid(qh[i]) * (D_HEAD ** -0.5) for i in n]
        vb = [hp[rows[i], _hs(R_IH, h)].astype(BF16) for i, (h, bb) in enumerate(probs_h)]
        split = [_split_bf16(logf[i]) for i in n]
        b = [_dot(tril_t, jnp.concatenate(split[i], axis=0)) for i in n]
        fill(2)
        bm = [b[i] - block_row(b[i], CH, CH // 2 - 1) for i in n]
        q_in = [(q[i] * jnp.exp(jnp.minimum(bm[i], EXP_CLAMP))).astype(BF16) for i in n]
        k_in = [(k[i] * jnp.exp(jnp.minimum(-bm[i], EXP_CLAMP))).astype(BF16) for i in n]
        a = [jnp.where(near_t, _dot_nt(q_in[i], k_in[i]), 0.0) for i in n]
        for w, mask in zip(widths, level_mask):
            hw = w // 2
            q_w, k_w = [], []
            for i in n:
                b3, q3, k3 = (x.reshape(T // w, w, D_HEAD) for x in (b[i], q[i], k[i]))
                ref = b3[:, hw - 1:hw, :]
                q_up = (q3[:, hw:, :] * jnp.exp(b3[:, hw:, :] - ref)).astype(BF16)
                k_lo = (k3[:, :hw, :] * jnp.exp(ref - b3[:, :hw, :])).astype(BF16)
                zero = jnp.zeros((T // w, hw, D_HEAD), BF16)
                q_w.append(jnp.concatenate([zero, q_up], axis=1).reshape(T, D_HEAD))
                k_w.append(jnp.concatenate([k_lo, zero], axis=1).reshape(T, D_HEAD))
            a = [jnp.where(mask, _dot_nt(q_w[i], k_w[i]), a[i]) for i in n]
        fill(2)
        b_end = [b[i][T - 1:T, :] for i in n]
        st = [sht[bb, h] for h, bb in probs_h]
        q_st = [(q[i] * jnp.exp(b[i])).astype(BF16) for i in n]
        k_st = [(k[i] * jnp.exp(b_end[i] - b[i])).astype(BF16) for i in n]
        o = [_dot(a[i].astype(BF16), vb[i]) + _dot_nt(q_st[i], st[i].astype(BF16)) for i in n]
        upd = [_dot_tn(vb[i], k_st[i]) for i in n]
        for i, (h, bb) in enumerate(probs_h):
            sht[bb, h] = st[i] * jnp.exp(b_end[i]) + upd[i]
            ms = jnp.mean(o[i] * o[i], axis=-1, keepdims=True)
            gate = hp[rows[i], _hs(R_GH, h)]
            ob_ref[bb, :, _hs(0, h)] = (o[i] * lax.rsqrt(ms + RMS_EPS) * hnw * _sigmoid(gate)).astype(BF16)

    neg_a = -jnp.exp(alog_ref[...])
    dtb = dtb_ref[...]
    gnw = gnw_ref[...]
    lane_g = lax.broadcasted_iota(jnp.int32, (CG, 128), 1)
    row_g = lax.broadcasted_iota(jnp.int32, (CG, CG), 0)
    col_g = lax.broadcasted_iota(jnp.int32, (CG, CG), 1)
    incl_g = row_g >= col_g
    strict_g = row_g > col_g
    tril_g = jnp.concatenate([incl_g.astype(BF16)] * 2, axis=1)
    n_neumann = int(math.log2(CG)) - 1

    NB = min(GDN_BATCH, R // CG)
    probs = [(j, h) for j in range(NB) for h in heads]

    def gdn_prepare(it):
        rows, gc, gc_t, egc, e_to_end, beta = [], [], [], [], [], []
        for j in range(NB):
            c = it * NB + j
            rows.append(slice(c * CG, (c + 1) * CG))
            ab = pr[rows[j], R_AB:R_AB + 128]
            xa = ab + dtb
            softplus = jnp.maximum(xa, 0.0) + jnp.log(1.0 + jnp.exp(-jnp.abs(xa)))
            g = jnp.where(lane_g < N_HEADS, neg_a * softplus, 0.0)
            beta.append(_sigmoid(ab))
            gcj = _dot(tril_g, jnp.concatenate(_split_bf16(g), axis=0))
            gc.append(gcj)
            gc_sq = jnp.concatenate([gcj, jnp.zeros((128 - CG, 128), F32)], axis=0) if CG < 128 else gcj
            gc_t.append(gc_sq.T)
            egc.append(jnp.exp(gcj))
            g_last = gcj[CG - 1:CG, :]
            e_to_end.append(jnp.exp(g_last - gcj))
            els[c * 8:(c + 1) * 8, :] = jnp.broadcast_to(jnp.exp(g_last), (8, 128))

        q = [qkv[Q_SLAB + h, rows[j], :] for j, h in probs]
        k = [qkv[K_SLAB + h, rows[j], :] for j, h in probs]
        v = [qkv[V_SLAB + h, rows[j], :] for j, h in probs]
        b_col = [beta[j][:, N_HEADS + h:N_HEADS + h + 1] for j, h in probs]
        eg_col = [egc[j][:, h:h + 1] for j, h in probs]
        dec = [jnp.exp(jnp.minimum(gc[j][:, h:h + 1] - gc_t[j][h:h + 1, 0:CG], 0.0)) for j, h in probs]
        n = range(len(probs))
        kb = [k[i] * b_col[i] for i in n]
        m = [_dot_nt(jnp.concatenate([kb[i], q[i]], axis=0).astype(BF16), k[i].astype(BF16)) for i in n]
        lp = [jnp.where(strict_g, m[i][0:CG] * dec[i], 0.0) for i in n]
        for i, (j, h) in enumerate(probs):
            aqs[rows[j], h * D_HEAD:h * D_HEAD + CG] = jnp.where(incl_g, m[i][CG:2 * CG] * dec[i], 0.0)
        p = [-lp[i] for i in n]
        lpb = [lp[i].astype(BF16) for i in n]
        for step in range(n_neumann):
            fill()
            lp = [_dot(lpb[i], lpb[i]) for i in n]
            lpb = [lp[i].astype(BF16) for i in n]
            p = [p[i] + lp[i] + _dot(p[i].astype(BF16), lpb[i]) for i in n]
        rhs = [jnp.concatenate([v[i] * b_col[i], kb[i] * eg_col[i]], axis=1) for i in n]
        sol = [rhs[i] + _dot(p[i].astype(BF16), rhs[i].astype(BF16)) for i in n]
        for i, (j, h) in enumerate(probs):
            qkv[V_SLAB + h, rows[j], :] = sol[i][:, 0:D_HEAD]
            kcs[rows[j], _hs(0, h)] = sol[i][:, D_HEAD:2 * D_HEAD]
            qkv[Q_SLAB + h, rows[j], :] = q[i] * eg_col[i]
            qkv[K_SLAB + h, rows[j], :] = k[i] * e_to_end[j][:, h:h + 1]

    assert (R // CG) % NB == 0
    for it in range((R // CG) // NB):
        gdn_prepare(it)

    rec = [(bb, h) for bb in seqs for h in heads]

    def gdn_recur(c):
        need("gate_a")
        n = range(len(rec))
        rows = [slice(bb * T + c * CG, bb * T + (c + 1) * CG) for bb, h in rec]
        e_rows = [els[(bb * (T // CG) + c) * 8:(bb * (T // CG) + c) * 8 + 1, :] for bb in seqs]
        e_last = [e_rows[bb][:, h:h + 1] for bb, h in rec]
        s = [sg_ref[bb, h] for bb, h in rec]
        lhs = [jnp.concatenate([kcs[rows[i], _hs(0, h)], qkv[Q_SLAB + h, rows[i], :]], axis=0).astype(BF16)
               for i, (bb, h) in enumerate(rec)]
        qs = [_dot(lhs[i], s[i].astype(BF16)) for i in n]
        fill()
        v_new = [(qkv[V_SLAB + h, rows[i], :] - qs[i][0:CG]).astype(BF16) for i, (bb, h) in enumerate(rec)]
        o = [qs[i][CG:2 * CG] + _dot(aqs[rows[i], h * D_HEAD:h * D_HEAD + CG].astype(BF16), v_new[i])
             for i, (bb, h) in enumerate(rec)]
        upd = [_dot_tn(qkv[K_SLAB + h, rows[i], :].astype(BF16), v_new[i]) for i, (bb, h) in enumerate(rec)]
        fill()
        for i, (bb, h) in enumerate(rec):
            sg_ref[bb, h] = s[i] * e_last[i] + upd[i]
            ms = jnp.mean(o[i] * o[i], axis=-1, keepdims=True)
            gate = pr[rows[i], _hs(R_GA, h)]
            oa_ref[bb, c * CG:(c + 1) * CG, _hs(0, h)] = (
                o[i] * lax.rsqrt(ms + RMS_EPS) * gnw * (gate * _sigmoid(gate))).astype(BF16)

    n_rec = T // CG
    for p in range(N_HEADS // 2):
        hgrn_pair(p)
        if p < n_rec:
            gdn_recur(p)
    for c in range(N_HEADS // 2, n_rec):
        gdn_recur(c)
    fill(len(pending))

    @pl.when(i == nt - 1)
    def _hgrn_state():
        for bb in seqs:
            for h in heads:
                sh_ref[bb, h] = sht[bb, h].T


def _ffn_kernel(FC, NP, x_ref, oa_ref, ob_ref, mg_ref, wa_ref, wb_ref, wo_ref, g1_ref, b1_ref,
                wgu_ref, wd_ref, g2_ref, b2_ref, y_ref):
    rows_per = x_ref.shape[0] // NP
    parts = range(NP)
    rows = [slice(p * rows_per, (p + 1) * rows_per) for p in parts]
    ya = [_dot(oa_ref[rows[p], :], wa_ref[...]) for p in parts]
    yb = [_dot(ob_ref[rows[p], :], wb_ref[...]) for p in parts]
    merged = [(_sigmoid(mg_ref[rows[p], 0:D_MODEL].astype(F32)) * ya[p]
               + _sigmoid(mg_ref[rows[p], D_MODEL:2 * D_MODEL].astype(F32)) * yb[p]).astype(BF16) for p in parts]
    mix = [_dot(merged[p], wo_ref[...]) for p in parts]
    x1 = [_layer_norm(ALPHA * x_ref[rows[p], :] + mix[p], g1_ref[...], b1_ref[...]) for p in parts]
    x1b = [x1[p].astype(BF16) for p in parts]
    acc = [jnp.zeros(x1[p].shape, F32) for p in parts]
    for j in range(D_FF // FC):
        gate = [_dot(x1b[p], wgu_ref[:, j * FC:(j + 1) * FC]) for p in parts]
        up = [_dot(x1b[p], wgu_ref[:, D_FF + j * FC:D_FF + (j + 1) * FC]) for p in parts]
        act = [(gate[p] * _sigmoid(gate[p]) * up[p]).astype(BF16) for p in parts]
        acc = [acc[p] + _dot(act[p], wd_ref[j * FC:(j + 1) * FC, :]) for p in parts]
    for p in parts:
        y_ref[rows[p], :] = _layer_norm(ALPHA * x1[p] + acc[p], g2_ref[...], b2_ref[...])


def _const_spec(shape):
    nd = len(shape)
    return pl.BlockSpec(shape, lambda *_: (0,) * nd, pipeline_mode=pl.Buffered(1))


def _tiles(B, L):
    T = min(128, L)
    if L == T and B * T <= 128:
        BB = B
    else:
        BB = 2 if B % 2 == 0 else 1
    assert L % T == 0 and L >= CONV_W - 1 and T % 8 == 0
    CG = min(64, T)
    CH = min(16, T)
    assert T % CG == 0 and T % CH == 0 and CG & (CG - 1) == 0 and CH & (CH - 1) == 0 and CH >= 8
    return BB, T, CG, CH


def _layer(layer, x, conv_buf, s_gdn, s_hgrn, w_in_p, conv_w, a_log, dt_bias, gdn_norm_w, lb_logits,
           hgrn_norm_w, w_br_a, w_br_b, w_out, ln1_g, ln1_b, w_gate_up, w_down, ln2_g, ln2_b):
    B, L, _ = x.shape
    BB, T, CG, CH = _tiles(B, L)
    nt = L // T
    R = BB * T
    cache8 = jnp.pad(conv_buf.astype(F32), ((0, 0), (8 - (CONV_W - 1), 0), (0, 0)))
    row128 = lambda v: jnp.pad(v.astype(F32).reshape(1, -1), ((0, 0), (0, 128 - v.shape[-1])))

    seq = lambda w: pl.BlockSpec((BB, T, w), lambda b, i: (b, i, 0))
    per_b = lambda *s: pl.BlockSpec((BB,) + s, lambda b, i: (b,) + (0,) * len(s))
    oa, ob, mg, conv_new, sg_new, sh_new = pl.pallas_call(
        functools.partial(_mixer_kernel, layer, BB, T, CG, CH, nt),
        grid=(B // BB, nt),
        in_specs=[seq(D_MODEL), per_b(8, CONV_CH), per_b(N_HEADS, D_HEAD, D_HEAD), per_b(N_HEADS, D_HEAD, D_HEAD),
                  _const_spec((D_MODEL, P_TOTAL)), _const_spec((CONV_W, CONV_CH)), _const_spec((1, 128)),
                  _const_spec((1, 128)), _const_spec((1, 128)), _const_spec(lb_logits.shape),
                  _const_spec((1, 128))],
        out_specs=[seq(D_MODEL), seq(D_MODEL), seq(2 * D_MODEL), per_b(CONV_W - 1, CONV_CH),
                   per_b(N_HEADS, D_HEAD, D_HEAD), per_b(N_HEADS, D_HEAD, D_HEAD)],
        out_shape=[jax.ShapeDtypeStruct((B, L, D_MODEL), BF16), jax.ShapeDtypeStruct((B, L, D_MODEL), BF16),
                   jax.ShapeDtypeStruct((B, L, 2 * D_MODEL), BF16),
                   jax.ShapeDtypeStruct((B, CONV_W - 1, CONV_CH), F32),
                   jax.ShapeDtypeStruct((B, N_HEADS, D_HEAD, D_HEAD), F32),
                   jax.ShapeDtypeStruct((B, N_HEADS, D_HEAD, D_HEAD), F32)],
        scratch_shapes=[pltpu.VMEM((R, D_MODEL), BF16),
                        pltpu.VMEM((BB, N_SLABS, T + 8, 128), F32),
                        pltpu.VMEM((N_SLABS, R, 128), F32),
                        pltpu.VMEM((R, R_TOTAL), F32),
                        pltpu.VMEM((R, 4 * D_MODEL), F32),
                        pltpu.VMEM((R, D_MODEL), F32),
                        pltpu.VMEM((R, D_MODEL), F32),
                        pltpu.VMEM((8 * (R // CG), 128), F32),
                        pltpu.VMEM((BB, N_HEADS, D_HEAD, D_HEAD), F32)],
        compiler_params=pltpu.CompilerParams(dimension_semantics=("parallel", "arbitrary"),
                                             vmem_limit_bytes=VMEM_LIMIT_BYTES),
        name="mixer",
    )(x, cache8, s_gdn.astype(F32), s_hgrn.astype(F32), w_in_p, conv_w.astype(F32), row128(a_log),
      row128(dt_bias), gdn_norm_w.astype(F32).reshape(1, -1), lb_logits.astype(F32),
      hgrn_norm_w.astype(F32).reshape(1, -1))

    N = B * L
    T2 = min(512, N)
    assert N % T2 == 0
    FC = 2816
    NP = 2 if T2 % 512 == 0 else 1
    tok = lambda w: pl.BlockSpec((T2, w), lambda i: (i, 0))
    row = lambda v: v.astype(F32).reshape(1, -1)
    y = pl.pallas_call(
        functools.partial(_ffn_kernel, FC, NP),
        grid=(N // T2,),
        in_specs=[tok(D_MODEL), tok(D_MODEL), tok(D_MODEL), tok(2 * D_MODEL),
                  _const_spec((D_MODEL, D_MODEL)), _const_spec((D_MODEL, D_MODEL)), _const_spec((D_MODEL, D_MODEL)),
                  _const_spec((1, D_MODEL)), _const_spec((1, D_MODEL)),
                  _const_spec((D_MODEL, 2 * D_FF)), _const_spec((D_FF, D_MODEL)),
                  _const_spec((1, D_MODEL)), _const_spec((1, D_MODEL))],
        out_specs=tok(D_MODEL),
        out_shape=jax.ShapeDtypeStruct((N, D_MODEL), F32),
        compiler_params=pltpu.CompilerParams(dimension_semantics=("parallel",),
                                             vmem_limit_bytes=VMEM_LIMIT_BYTES),
        name="ffn",
    )(x.reshape(N, D_MODEL), oa.reshape(N, D_MODEL), ob.reshape(N, D_MODEL), mg.reshape(N, 2 * D_MODEL),
      w_br_a, w_br_b, w_out, row(ln1_g), row(ln1_b), w_gate_up, w_down, row(ln2_g), row(ln2_b))
    return y.reshape(B, L, D_MODEL), conv_new, sg_new, sh_new


def _permute_w_in(w):
    n_ab = 2 * N_HEADS
    w = w.astype(BF16)
    head = w[:, 0:4 * D_MODEL]
    ab = w[:, 4 * D_MODEL:4 * D_MODEL + n_ab]
    tail = w[:, 4 * D_MODEL + n_ab:]
    pad = jnp.zeros((w.shape[0], 128 - n_ab), w.dtype)
    return jnp.concatenate([head, tail, ab, pad], axis=1)


def kernel(x_prompt, x_sample, cache_gdn_conv, state_gdn, state_hgrn, w_in, conv_w, a_log, dt_bias, gdn_norm_w, hgrn_lb_logits, hgrn_norm_w, w_br_a, w_br_b, w_out, ln1_g, ln1_b, w_gate_up, w_down, ln2_g, ln2_b):
    depth = w_in.shape[0]
    assert depth == 1, "ALPHA is baked for a single layer"
    B = x_prompt.shape[0]
    dt = x_prompt.dtype
    y_p, y_s = x_prompt, x_sample
    outs = [[] for _ in range(6)]
    for l in range(depth):
        wl = (_permute_w_in(w_in[l]), conv_w[l], a_log[l], dt_bias[l], gdn_norm_w[l], hgrn_lb_logits,
              hgrn_norm_w[l], w_br_a[l].astype(BF16), w_br_b[l].astype(BF16), w_out[l].astype(BF16),
              ln1_g[l], ln1_b[l], w_gate_up[l].astype(BF16), w_down[l].astype(BF16), ln2_g[l], ln2_b[l])
        zc = jnp.zeros((B, CONV_W - 1, CONV_CH), F32)
        zs = jnp.zeros((B, N_HEADS, D_HEAD, D_HEAD), F32)
        y_p, cp, gp, hp = _layer(l, y_p, zc, zs, zs, *wl)
        y_s, cs, gs, hs = _layer(l, y_s, cache_gdn_conv[l], state_gdn[l], state_hgrn[l], *wl)
        for lst, v in zip(outs, (cp, gp, hp, cs, gs, hs)):
            lst.append(v.astype(dt))
    return (y_p.astype(dt), y_s.astype(dt)) + tuple(jnp.stack(o) for o in outs)
```
